```python
import math
import jax
import jax.numpy as jnp
from jax import lax
import numpy as np

D_MODEL = 4096
BATCH = 2
SEQ = 8192
DEPTH = 2

GRID_W = 64
CTX_LEN = 256
CHUNK = 128
CONV_W = 5
EPS = 1e-6
F32 = jnp.float32

SSD_HEADS = 32
SSD_HEAD_DIM = 64
SSD_INNER = SSD_HEADS * SSD_HEAD_DIM
SSD_GROUPS = 4
SSD_GROUP_HEADS = SSD_HEADS // SSD_GROUPS
SSD_STATE = 128
SSD_CONV_DIM = SSD_INNER + 2 * SSD_GROUPS * SSD_STATE

RET_HEADS = 16
RET_QK_DIM = 64
RET_V_DIM = 128
RET_QK = RET_HEADS * RET_QK_DIM
RET_V = RET_HEADS * RET_V_DIM
ROPE_BASE = 10000.0

EVEN_SPLITS = (SSD_INNER, SSD_CONV_DIM, 2 * SSD_HEADS, RET_QK, RET_QK, RET_V, RET_V)
EVEN_IN = sum(EVEN_SPLITS)
EVEN_MIX = SSD_INNER + RET_V

GDN_K_HEADS = 16
GDN_V_HEADS = 32
GDN_HEAD_DIM = 128
GDN_K = GDN_K_HEADS * GDN_HEAD_DIM
GDN_V = GDN_V_HEADS * GDN_HEAD_DIM
GDN_CONV_DIM = 2 * GDN_K + GDN_V
ODD_SPLITS = (GDN_CONV_DIM, GDN_V, 2 * GDN_V_HEADS, 2 * GDN_V_HEADS)
ODD_IN = sum(ODD_SPLITS)

D_FF = -(-8 * D_MODEL // (3 * 256)) * 256

kernel_name = "hybrid_ssd_retention_gdn_prefix_backbone"


def _split(t, sizes):
    return jnp.split(t, np.cumsum(sizes)[:-1].tolist(), axis=-1)


def rmsnorm(t, w):
    t32 = t.astype(F32)
    y = t32 * lax.rsqrt(jnp.mean(t32 * t32, axis=-1, keepdims=True) + EPS)
    return (y * w.astype(F32)).astype(t.dtype)


def l2norm(t):
    t32 = t.astype(F32)
    return (t32 * lax.rsqrt(jnp.sum(t32 * t32, axis=-1, keepdims=True) + EPS)).astype(t.dtype)


def head_layernorm(t):
    t32 = t.astype(F32)
    mu = jnp.mean(t32, axis=-1, keepdims=True)
    var = jnp.mean(jnp.square(t32 - mu), axis=-1, keepdims=True)
    return ((t32 - mu) * lax.rsqrt(var + EPS)).astype(t.dtype)


def modulate(h, shift, scale):
    return h * (1.0 + scale) + shift


def dwconv(t, w):
    ch = t.shape[-1]
    return lax.conv_general_dilated(
        t, w[:, None, :].astype(t.dtype), window_strides=(1,),
        padding=[(CONV_W // 2, CONV_W // 2)],
        dimension_numbers=("NWC", "WIO", "NWC"), feature_group_count=ch)


def axial_rotary(t):
    l = t.shape[1]
    rows = l // GRID_W
    row = jnp.repeat(jnp.arange(rows, dtype=F32), GRID_W)
    col = jnp.tile(jnp.arange(GRID_W, dtype=F32), rows)
    nf = RET_QK_DIM // 4
    freqs = ROPE_BASE ** (-jnp.arange(nf, dtype=F32) / nf)
    ang = jnp.concatenate([row[:, None] * freqs, col[:, None] * freqs], axis=-1)
    cos = jnp.cos(ang)[None, :, None, :]
    sin = jnp.sin(ang)[None, :, None, :]
    half = RET_QK_DIM // 2
    t1, t2 = t[..., :half], t[..., half:]
    return jnp.concatenate([t1 * cos - t2 * sin, t2 * cos + t1 * sin], axis=-1).astype(t.dtype)


def decay_scan(q, k, v, log_a, h0):
    b, l, g, dk = q.shape
    r, dv = v.shape[-2:]
    n = l // CHUNK
    qc = q.reshape(b, n, CHUNK, g, dk)
    kc = k.reshape(b, n, CHUNK, g, dk)
    vc = v.reshape(b, n, CHUNK, g, r, dv)
    cs = jnp.cumsum(log_a.astype(F32).reshape(b, n, CHUNK, g, r), axis=2)
    lower = jnp.tril(jnp.ones((CHUNK, CHUNK), bool))[None, None, :, :, None, None]
    decay = jnp.exp(jnp.where(lower, cs[:, :, :, None] - cs[:, :, None, :], -jnp.inf))
    scores = jnp.einsum("bnigk,bnjgk->bnijg", qc, kc)
    y = jnp.einsum("bnijgr,bnjgrv->bnigrv", scores[..., None] * decay, vc)
    v_end = vc * jnp.exp(cs[:, :, -1:] - cs)[..., None]
    states = jnp.einsum("bncgk,bncgrv->bngrkv", kc, v_end)
    chunk_decay = jnp.exp(cs[:, :, -1])

    def carry_chunk(h, inp):
        st, dec = inp
        return h * dec[..., None, None] + st, h

    h_last, h_in = lax.scan(carry_chunk, h0.astype(states.dtype),
                            (jnp.moveaxis(states, 1, 0), jnp.moveaxis(chunk_decay, 1, 0)))
    h_in = jnp.moveaxis(h_in, 0, 1)
    y = y + jnp.einsum("bnigk,bngrkv->bnigrv", qc, h_in) * jnp.exp(cs)[..., None]
    return y.reshape(b, l, g, r, dv).astype(q.dtype), h_last


def delta_scan(q, k, v, g, beta, h0):
    b, l, h, dk = q.shape
    dv = v.shape[-1]
    n = l // CHUNK

    def to_chunks(t):
        return jnp.moveaxis(t.reshape((b, n, CHUNK, h) + t.shape[3:]), 3, 2)

    qc, kc, vc = to_chunks(q), to_chunks(k), to_chunks(v)
    bc = to_chunks(beta.astype(F32))
    cs = jnp.cumsum(to_chunks(g.astype(F32)), axis=-1)
    incl = jnp.tril(jnp.ones((CHUNK, CHUNK), bool))
    strict = jnp.tril(jnp.ones((CHUNK, CHUNK), bool), -1)
    gam = jnp.exp(jnp.where(incl, cs[..., :, None] - cs[..., None, :], -jnp.inf))
    kk = jnp.einsum("bnhik,bnhjk->bnhij", kc, kc)
    a_mat = jnp.where(strict, kk * gam * bc[..., :, None], 0.0) + jnp.eye(CHUNK, dtype=F32)
    rhs = jnp.concatenate([vc * bc[..., None], kc * (bc * jnp.exp(cs))[..., None]], axis=-1)
    sol = lax.linalg.triangular_solve(a_mat, rhs.astype(a_mat.dtype), left_side=True, lower=True,
                                      unit_diagonal=True)
    u, w = sol[..., :dv], sol[..., dv:]
    att = jnp.einsum("bnhik,bnhjk->bnhij", qc, kc) * gam

    def step(s, inp):
        q_i, k_i, u_i, w_i, cs_i, att_i = inp
        v_new = u_i - jnp.einsum("bhck,bhkv->bhcv", w_i, s)
        o = (jnp.einsum("bhck,bhkv->bhcv", q_i * jnp.exp(cs_i)[..., None], s)
             + jnp.einsum("bhij,bhjv->bhiv", att_i, v_new))
        k_end = k_i * jnp.exp(cs_i[..., -1:] - cs_i)[..., None]
        s = s * jnp.exp(cs_i[..., -1])[..., None, None] + jnp.einsum("bhck,bhcv->bhkv", k_end, v_new)
        return s, o

    xs = tuple(jnp.moveaxis(t, 1, 0) for t in (qc, kc, u, w, cs, att))
    s_last, o = lax.scan(step, h0.astype(F32), xs)
    o = jnp.moveaxis(jnp.moveaxis(o, 0, 1), 2, 3).reshape(b, l, h, dv)
    return o.astype(v.dtype), s_last


def two_pass(scan_fn, ctx_args, lat_args, h0, reverse):
    if reverse:
        ctx_args = tuple(jnp.flip(a, axis=1) for a in ctx_args)
        lat_args = tuple(jnp.flip(a, axis=1) for a in lat_args)
    y_c, h_c = scan_fn(*ctx_args, h0)
    y_x, _ = scan_fn(*lat_args, h_c)
    if reverse:
        y_c, y_x = jnp.flip(y_c, axis=1), jnp.flip(y_x, axis=1)
    return y_c, y_x


def even_features(h, w_in, conv_w, conv_b, dt_bias, a_log, rotary):
    b, l, _ = h.shape
    z, xbc, dt, rq, rk, rv, rg = _split(h @ w_in, EVEN_SPLITS)
    xbc = jax.nn.silu(dwconv(xbc, conv_w) + conv_b)
    xs, bm, cm = _split(xbc, (SSD_INNER, SSD_GROUPS * SSD_STATE, SSD_GROUPS * SSD_STATE))
    xs = xs.reshape(b, l, SSD_GROUPS, SSD_GROUP_HEADS, SSD_HEAD_DIM)
    bm = bm.reshape(b, l, SSD_GROUPS, SSD_STATE)
    cm = cm.reshape(b, l, SSD_GROUPS, SSD_STATE)
    dt = jax.nn.softplus(dt.reshape(b, l, 2, SSD_HEADS).astype(F32) + dt_bias.astype(F32))
    la = (-jnp.exp(a_log.astype(F32)) * dt).reshape(b, l, 2, SSD_GROUPS, SSD_GROUP_HEADS)
    dt = dt.reshape(b, l, 2, SSD_GROUPS, SSD_GROUP_HEADS)
    v_dirs = [(xs * dt[:, :, d][..., None]).astype(xs.dtype) for d in range(2)]
    la_dirs = [la[:, :, d] for d in range(2)]
    rq = rq.reshape(b, l, RET_HEADS, RET_QK_DIM)
    rk = rk.reshape(b, l, RET_HEADS, RET_QK_DIM) * RET_QK_DIM ** -0.5
    if rotary:
        rq, rk = axial_rotary(rq), axial_rotary(rk)
    rv = rv.reshape(b, l, RET_HEADS, 1, RET_V_DIM)
    return {"z": z, "xs": xs, "bm": bm, "cm": cm, "v": v_dirs, "la": la_dirs,
            "rq": rq, "rk": rk, "rv": rv, "rg": rg}


def even_out(f, y_ssd, y_ret, d_skip, ssd_norm, w_out):
    b, l = y_ssd.shape[:2]
    y = y_ssd + d_skip.reshape(SSD_GROUPS, SSD_GROUP_HEADS)[..., None] * f["xs"]
    y = y.reshape(b, l, SSD_INNER) * jax.nn.silu(f["z"])
    gsz = SSD_INNER // SSD_GROUPS
    y = rmsnorm(y.reshape(b, l, SSD_GROUPS, gsz), ssd_norm.reshape(SSD_GROUPS, gsz)).reshape(b, l, SSD_INNER)
    r = head_layernorm(y_ret[:, :, :, 0]).reshape(b, l, RET_V) * jax.nn.silu(f["rg"])
    return jnp.concatenate([y, r], axis=-1) @ w_out


def even_mixer(hc, hx, w_in, conv_w, conv_b, dt_bias, a_log, d_skip, ssd_norm, ret_decay, w_out, ctx_out):
    fc = even_features(hc, w_in, conv_w, conv_b, dt_bias, a_log, False)
    fx = even_features(hx, w_in, conv_w, conv_b, dt_bias, a_log, True)
    b = hx.shape[0]
    log_gamma = jax.nn.log_sigmoid(ret_decay.astype(F32))
    ssd_h0 = jnp.zeros((b, SSD_GROUPS, SSD_GROUP_HEADS, SSD_STATE, SSD_HEAD_DIM), F32)
    ret_h0 = jnp.zeros((b, RET_HEADS, 1, RET_QK_DIM, RET_V_DIM), F32)
    ssd_c = ssd_x = ret_c = ret_x = 0.0
    for d in range(2):
        yc, yx = two_pass(decay_scan, (fc["cm"], fc["bm"], fc["v"][d], fc["la"][d]),
                          (fx["cm"], fx["bm"], fx["v"][d], fx["la"][d]), ssd_h0, d == 1)
        ssd_c, ssd_x = ssd_c + yc, ssd_x + yx
        la_c = jnp.broadcast_to(log_gamma[d][:, None], fc["rq"].shape[:3] + (1,))
        la_x = jnp.broadcast_to(log_gamma[d][:, None], fx["rq"].shape[:3] + (1,))
        yc, yx = two_pass(decay_scan, (fc["rq"], fc["rk"], fc["rv"], la_c),
                          (fx["rq"], fx["rk"], fx["rv"], la_x), ret_h0, d == 1)
        ret_c, ret_x = ret_c + yc, ret_x + yx
    out_x = even_out(fx, ssd_x, ret_x, d_skip, ssd_norm, w_out)
    out_c = even_out(fc, ssd_c, ret_c, d_skip, ssd_norm, w_out) if ctx_out else None
    return out_c, out_x


def odd_features(h, w_in, conv_w, dt_bias, a_log):
    b, l, _ = h.shape
    qkv, z, beta, a = _split(h @ w_in, ODD_SPLITS)
    qkv = jax.nn.silu(dwconv(qkv, conv_w))
    q, k, v = _split(qkv, (GDN_K, GDN_K, GDN_V))
    rep = GDN_V_HEADS // GDN_K_HEADS
    q = jnp.repeat(l2norm(q.reshape(b, l, GDN_K_HEADS, GDN_HEAD_DIM)) * GDN_HEAD_DIM ** -0.5, rep, axis=2)
    k = jnp.repeat(l2norm(k.reshape(b, l, GDN_K_HEADS, GDN_HEAD_DIM)), rep, axis=2)
    v = v.reshape(b, l, GDN_V_HEADS, GDN_HEAD_DIM)
    beta = jax.nn.sigmoid(beta.reshape(b, l, 2, GDN_V_HEADS).astype(F32))
    g = -jnp.exp(a_log.astype(F32)) * jax.nn.softplus(a.reshape(b, l, 2, GDN_V_HEADS).astype(F32)
                                                       + dt_bias.astype(F32))
    return {"q": q, "k": k, "v": v, "z": z, "beta": beta, "g": g}


def odd_out(f, o, norm_w, w_out):
    b, l = o.shape[:2]
    y = rmsnorm(o, norm_w) * jax.nn.silu(f["z"].reshape(b, l, GDN_V_HEADS, GDN_HEAD_DIM))
    return y.reshape(b, l, GDN_V) @ w_out


def odd_mixer(hc, hx, w_in, conv_w, dt_bias, a_log, norm_w, w_out, ctx_out):
    fc = odd_features(hc, w_in, conv_w, dt_bias, a_log)
    fx = odd_features(hx, w_in, conv_w, dt_bias, a_log)
    h0 = jnp.zeros((hx.shape[0], GDN_V_HEADS, GDN_HEAD_DIM, GDN_HEAD_DIM), F32)

    def args(f, d):
        return (f["q"], f["k"], f["v"], f["g"][:, :, d], f["beta"][:, :, d])

    o_c = o_x = 0.0
    for d in range(2):
        yc, yx = two_pass(delta_scan, args(fc, d), args(fx, d), h0, d == 1)
        o_c, o_x = o_c + yc, o_x + yx
    out_x = odd_out(fx, o_x, norm_w, w_out)
    out_c = odd_out(fc, o_c, norm_w, w_out) if ctx_out else None
    return out_c, out_x


def swiglu(h, w1, w3, w2):
    return (jax.nn.silu(h @ w1) * (h @ w3)) @ w2


def residual_tail(s, o, mod, post_mix, pre_ffn, post_ffn, w1, w3, w2):
    s = s + mod[2] * rmsnorm(o, post_mix)
    h = modulate(rmsnorm(s, pre_ffn), mod[3], mod[4])
    return s + mod[5] * rmsnorm(swiglu(h, w1, w3, w2), post_ffn)


def setup_inputs(seed: int = 0) -> dict:
    key = jax.random.key(seed)
    keys = iter(jax.random.split(key, 48))

    def normal(shape, scale):
        return jax.random.normal(next(keys), shape, F32) * scale

    def gain(shape):
        return 1.0 + normal(shape, 0.01)

    def dt_bias(shape):
        dt = jnp.exp(jax.random.uniform(next(keys), shape, F32, math.log(1e-3), math.log(1e-1)))
        return dt + jnp.log(-jnp.expm1(-dt))

    def a_log(shape):
        return jnp.log(jax.random.uniform(next(keys), shape, F32, 1.0, 16.0))

    n_ev = (DEPTH + 1) // 2
    n_od = DEPTH // 2
    d = D_MODEL
    ret_logit = jnp.log(2.0 ** (5.0 + jnp.arange(RET_HEADS, dtype=F32)) - 1.0)
    return {
        "x": normal((BATCH, SEQ, d), 1.0),
        "c": normal((BATCH, d), 1.0),
        "ctx": normal((BATCH, CTX_LEN, d), 1.0),
        "c_ctx": normal((d,), 1.0),
        "ada_w": normal((DEPTH, d, 6 * d), 0.5 * d ** -0.5),
        "ada_b": normal((DEPTH, 6 * d), 0.01),
        "norm_mix_pre": gain((DEPTH, d)),
        "norm_mix_post": gain((DEPTH, d)),
        "norm_ffn_pre": gain((DEPTH, d)),
        "norm_ffn_post": gain((DEPTH, d)),
        "ev_w_in": normal((n_ev, d, EVEN_IN), d ** -0.5),
        "ev_conv_w": normal((n_ev, CONV_W, SSD_CONV_DIM), CONV_W ** -0.5),
        "ev_conv_b": normal((n_ev, SSD_CONV_DIM), 0.01),
        "ev_dt_bias": dt_bias((n_ev, 2, SSD_HEADS)),
        "ev_a_log": a_log((n_ev, 2, SSD_HEADS)),
        "ev_d_skip": gain((n_ev, SSD_HEADS)),
        "ev_ssd_norm": gain((n_ev, SSD_INNER)),
        "ev_ret_decay": ret_logit[None, None, :] + normal((n_ev, 2, RET_HEADS), 0.01),
        "ev_w_out": normal((n_ev, EVEN_MIX, d), EVEN_MIX ** -0.5),
        "od_w_in": normal((n_od, d, ODD_IN), d ** -0.5),
        "od_conv_w": normal((n_od, CONV_W, GDN_CONV_DIM), CONV_W ** -0.5),
        "od_dt_bias": dt_bias((n_od, 2, GDN_V_HEADS)),
        "od_a_log": a_log((n_od, 2, GDN_V_HEADS)),
        "od_norm": gain((n_od, GDN_HEAD_DIM)),
        "od_w_out": normal((n_od, GDN_V, d), GDN_V ** -0.5),
        "ffn_w1": normal((DEPTH, d, D_FF), d ** -0.5),
        "ffn_w3": normal((DEPTH, d, D_FF), d ** -0.5),
        "ffn_w2": normal((DEPTH, D_FF, d), D_FF ** -0.5),
    }


def reference(x, c, ctx, c_ctx, ada_w, ada_b, norm_mix_pre, norm_mix_post, norm_ffn_pre, norm_ffn_post,
              ev_w_in, ev_conv_w, ev_conv_b, ev_dt_bias, ev_a_log, ev_d_skip, ev_ssd_norm, ev_ret_decay,
              ev_w_out, od_w_in, od_conv_w, od_dt_bias, od_a_log, od_norm, od_w_out,
              ffn_w1, ffn_w3, ffn_w2):
    cx = ctx
    for i in range(DEPTH):
        last = i == DEPTH - 1
        j = i // 2
        mod_x = jnp.split((jax.nn.silu(c) @ ada_w[i] + ada_b[i])[:, None, :], 6, axis=-1)
        mod_c = jnp.split(jax.nn.silu(c_ctx) @ ada_w[i] + ada_b[i], 6, axis=-1)
        hx = modulate(rmsnorm(x, norm_mix_pre[i]), mod_x[0], mod_x[1])
        hc = modulate(rmsnorm(cx, norm_mix_pre[i]), mod_c[0], mod_c[1])
        if i % 2 == 0:
            oc, ox = even_mixer(hc, hx, ev_w_in[j], ev_conv_w[j], ev_conv_b[j], ev_dt_bias[j], ev_a_log[j],
                                ev_d_skip[j], ev_ssd_norm[j], ev_ret_decay[j], ev_w_out[j], not last)
        else:
            oc, ox = odd_mixer(hc, hx, od_w_in[j], od_conv_w[j], od_dt_bias[j], od_a_log[j],
                               od_norm[j], od_w_out[j], not last)
        x = residual_tail(x, ox, mod_x, norm_mix_post[i], norm_ffn_pre[i], norm_ffn_post[i],
                          ffn_w1[i], ffn_w3[i], ffn_w2[i])
        if not last:
            cx = residual_tail(cx, oc, mod_c, norm_mix_post[i], norm_ffn_pre[i], norm_ffn_post[i],
                               ffn_w1[i], ffn_w3[i], ffn_w2[i])
    return x
```

```python
import functools
import math

import jax
import jax.numpy as jnp
from jax import lax
from jax.experimental import pallas as pl
from jax.experimental.pallas import tpu as pltpu

F32 = jnp.float32
BF16 = jnp.bfloat16

EPS = 1e-6
CHUNK = 128
LANES = 128
CONV_W = 5
CONV_HALO = 8
GRID_W = 64
ROPE_BASE = 10000.0
ROW_TILE = 256
NEG_BIG = -1e30

SSD_HEADS = 32
SSD_HEAD_DIM = 64
SSD_GROUPS = 4
SSD_STATE = 128
SSD_INNER = SSD_HEADS * SSD_HEAD_DIM
RET_HEADS = 16
RET_QK_DIM = 64
RET_V_DIM = 128
RET_QK = RET_HEADS * RET_QK_DIM
RET_V = RET_HEADS * RET_V_DIM
GDN_K_HEADS = 16
GDN_V_HEADS = 32
GDN_HEAD_DIM = 128
GDN_K = GDN_K_HEADS * GDN_HEAD_DIM
GDN_V = GDN_V_HEADS * GDN_HEAD_DIM

VMEM_LIMIT = 56 * 1024 * 1024


def _cparams(sem):
    return pltpu.CompilerParams(dimension_semantics=sem, vmem_limit_bytes=VMEM_LIMIT)


def _dot(a, b):
    return jnp.dot(a, b, preferred_element_type=F32)


def _dot_nt(a, b):
    return lax.dot_general(a, b, (((1,), (1,)), ((), ())), preferred_element_type=F32)


def _split3(x):
    hi = x.astype(BF16)
    r1 = x - hi.astype(F32)
    mid = r1.astype(BF16)
    lo = (r1 - mid.astype(F32)).astype(BF16)
    return hi, mid, lo


def _dot_sel_left(sel, x):
    hi, mid, lo = _split3(x)
    return _dot(sel, hi) + _dot(sel, mid) + _dot(sel, lo)


def _dot_sel_right(x, sel):
    hi, mid, lo = _split3(x)
    return _dot(hi, sel) + _dot(mid, sel) + _dot(lo, sel)


def _silu(x):
    return x * jax.nn.sigmoid(x)


def _softplus(x):
    return jnp.maximum(x, 0.0) + jnp.log(1.0 + jnp.exp(-jnp.abs(x)))


def _rms(x, w):
    ms = jnp.mean(x * x, axis=-1, keepdims=True)
    return x * lax.rsqrt(ms + EPS) * w


ADA_TN = 512


def _ada_kernel(cb_ref, w_ref, b_ref, o_ref, act_ref, *, n_rows):
    first = jnp.logical_and(pl.program_id(0) == 0, pl.program_id(1) == 0)

    @pl.when(first)
    def _():
        act_ref[...] = _silu(cb_ref[...])

    d = w_ref.shape[1]
    tn = w_ref.shape[2]
    w3 = w_ref[0].reshape(d // 8, 8, tn)
    o_ref[...] = jnp.zeros_like(o_ref)
    for r in range(n_rows):
        a3 = act_ref[r].reshape(d // 8, 8, LANES)
        for lt in range(tn // LANES):
            sl = slice(lt * LANES, (lt + 1) * LANES)
            part = jnp.sum(w3[:, :, sl] * a3, axis=0)
            o_ref[0, r:r + 1, sl] = jnp.sum(part, axis=0, keepdims=True) + b_ref[0, :, sl]


def _ada(cvec, ada_w, ada_b):
    n_rows, d = cvec.shape
    depth, _, n = ada_w.shape
    cb = jnp.broadcast_to(cvec[:, :, None], (n_rows, d, LANES))
    return pl.pallas_call(
        functools.partial(_ada_kernel, n_rows=n_rows),
        grid=(depth, n // ADA_TN),
        in_specs=[
            pl.BlockSpec((n_rows, d, LANES), lambda i, j: (0, 0, 0)),
            pl.BlockSpec((1, d, ADA_TN), lambda i, j: (i, 0, j)),
            pl.BlockSpec((1, 1, ADA_TN), lambda i, j: (i, 0, j)),
        ],
        out_specs=pl.BlockSpec((1, 8, ADA_TN), lambda i, j: (i, 0, j)),
        out_shape=jax.ShapeDtypeStruct((depth, 8, n), F32),
        scratch_shapes=[pltpu.VMEM((n_rows, d, LANES), F32)],
        compiler_params=_cparams(("arbitrary", "arbitrary")),
        name="ada_mod",
    )(cb, ada_w, ada_b.reshape(depth, 1, n))


def _mod_row(m_ref, ctx_row):
    r = pl.program_id(0)
    if ctx_row is not None:
        r = jnp.where(pl.program_id(1) == 0, ctx_row, r)
    return m_ref[0, pl.ds(r, 1), :]


def _res_norm_kernel(*refs, has_res, has_h, ctx_row):
    it = iter(refs)
    s_ref = next(it)
    if has_res:
        o_ref, gate_ref, post_ref = next(it), next(it), next(it)
    if has_h:
        pre_ref, shift_ref, scale_ref = next(it), next(it), next(it)
    s = s_ref[0]
    if has_res:
        s = s + _mod_row(gate_ref, ctx_row) * _rms(o_ref[0], post_ref[...])
        next(it)[0] = s
    if has_h:
        h = _rms(s, pre_ref[...]) * (1.0 + _mod_row(scale_ref, ctx_row)) + _mod_row(shift_ref, ctx_row)
        next(it)[0] = h.astype(BF16)


def _res_norm(s, mod, *, o=None, gate=None, post_w=None, pre_w=None, shift=None, scale=None,
              row_off=0, out_rows=None):
    b, l, d = s.shape
    has_res = o is not None
    has_h = pre_w is not None
    n_tiles = (l // ROW_TILE - row_off) if out_rows is None else out_rows // ROW_TILE
    ctx_row = None if row_off else b

    def rows(bi, j):
        return (bi, j + row_off, 0)

    def outrows(bi, j):
        return (bi, j, 0)

    def mod_spec(sel):
        layer, comp = sel
        return pl.BlockSpec((1, 8, d), lambda bi, j: (layer, 0, comp))

    vec_spec = pl.BlockSpec((1, d), lambda bi, j: (0, 0))
    tile = (1, ROW_TILE, d)
    args, in_specs, out_specs, out_shape = [s], [pl.BlockSpec(tile, rows)], [], []
    if has_res:
        args += [o, mod, post_w.reshape(1, d)]
        in_specs += [pl.BlockSpec(tile, rows), mod_spec(gate), vec_spec]
        out_specs.append(pl.BlockSpec(tile, outrows))
        out_shape.append(jax.ShapeDtypeStruct((b, n_tiles * ROW_TILE, d), F32))
    if has_h:
        args += [pre_w.reshape(1, d), mod, mod]
        in_specs += [vec_spec, mod_spec(shift), mod_spec(scale)]
        out_specs.append(pl.BlockSpec(tile, outrows))
        out_shape.append(jax.ShapeDtypeStruct((b, n_tiles * ROW_TILE, d), BF16))
    outs = pl.pallas_call(
        functools.partial(_res_norm_kernel, has_res=has_res, has_h=has_h, ctx_row=ctx_row),
        grid=(b, n_tiles),
        in_specs=in_specs,
        out_specs=out_specs,
        out_shape=out_shape,
        compiler_params=_cparams(("parallel", "parallel")),
        name="res_norm",
    )(*args)
    outs = list(outs)
    s_new = outs.pop(0) if has_res else None
    h = outs.pop(0) if has_h else None
    return s_new, h


MM_TM = 1536


def _row_tile(m, cap=MM_TM):
    best = ROW_TILE
    for tm in range(ROW_TILE, cap + 1, ROW_TILE):
        if m % tm == 0:
            best = tm
    return best


def _mm_kernel(a_ref, *refs, swiglu):
    a = a_ref[...]
    if swiglu:
        w1_ref, w3_ref, o_ref = refs
        g = _dot(a, w1_ref[0].astype(BF16))
        u = _dot(a, w3_ref[0].astype(BF16))
        o_ref[...] = (_silu(g) * u).astype(o_ref.dtype)
    else:
        w_ref, o_ref = refs
        o_ref[...] = _dot(a, w_ref[0].astype(BF16)).astype(o_ref.dtype)


def _matmul(a, ws, *, layer=0, col_off=0, n, tn, out_dtype):
    m, k = a.shape
    swiglu = len(ws) == 2
    tm = _row_tile(m)
    assert col_off % tn == 0
    off = col_off // tn
    w_spec = pl.BlockSpec((1, k, tn), lambda i, j: (layer, 0, j + off))
    return pl.pallas_call(
        functools.partial(_mm_kernel, swiglu=swiglu),
        grid=(m // tm, pl.cdiv(n, tn)),
        in_specs=[pl.BlockSpec((tm, k), lambda i, j: (i, 0), pipeline_mode=pl.Buffered(1))]
        + [w_spec] * len(ws),
        out_specs=pl.BlockSpec((tm, tn), lambda i, j: (i, j)),
        out_shape=jax.ShapeDtypeStruct((m, n), out_dtype),
        compiler_params=_cparams(("parallel", "arbitrary")),
        name="matmul_swiglu" if swiglu else "matmul",
    )(a, *ws)


MMK_TN = 1024
MMK_TK = 1024


def _mm_ktiled_kernel(a_ref, w_ref, o_ref, acc_ref, *, k_total, tk):
    kk = pl.program_id(2)
    nk = pl.num_programs(2)

    @pl.when(kk == 0)
    def _():
        acc_ref[...] = jnp.zeros_like(acc_ref)

    rem = k_total - (pl.cdiv(k_total, tk) - 1) * tk

    def step(a, w):
        acc_ref[...] += _dot(a, w.astype(BF16))

    if rem == tk:
        step(a_ref[...], w_ref[0])
    else:
        @pl.when(kk < nk - 1)
        def _():
            step(a_ref[...], w_ref[0])

        @pl.when(kk == nk - 1)
        def _():
            a = a_ref[...]
            w = w_ref[0]
            ka = lax.broadcasted_iota(jnp.int32, a.shape, 1)
            kw = lax.broadcasted_iota(jnp.int32, w.shape, 0)
            step(jnp.where(ka < rem, a, jnp.zeros_like(a)), jnp.where(kw < rem, w, jnp.zeros_like(w)))

    @pl.when(kk == nk - 1)
    def _():
        o_ref[...] = acc_ref[...]


def _matmul_ktiled(a, w, *, layer, tk=MMK_TK):
    m, k = a.shape
    n = w.shape[2]
    tm = _row_tile(m)
    tn = min(MMK_TN, n)
    assert n % tn == 0
    return pl.pallas_call(
        functools.partial(_mm_ktiled_kernel, k_total=k, tk=tk),
        grid=(m // tm, n // tn, pl.cdiv(k, tk)),
        in_specs=[pl.BlockSpec((tm, tk), lambda i, j, kk: (i, kk)),
                  pl.BlockSpec((1, tk, tn), lambda i, j, kk: (layer, kk, j))],
        out_specs=pl.BlockSpec((tm, tn), lambda i, j, kk: (i, j)),
        out_shape=jax.ShapeDtypeStruct((m, n), F32),
        scratch_shapes=[pltpu.VMEM((tm, tn), F32)],
        compiler_params=_cparams(("parallel", "parallel", "arbitrary")),
        name="matmul_ktiled",
    )(a, w)


CONV_CW = 1024


def _conv_kernel(cur_ref, prev_ref, next_ref, w_ref, *refs, has_bias, l2_blocks, l2_scaled_blocks,
                 l2_scale):
    if has_bias:
        b_ref, o_ref, buf_ref = refs
    else:
        o_ref, buf_ref = refs
    j = pl.program_id(1)
    nj = pl.num_programs(1)
    tr = cur_ref.shape[1]
    use_prev = jnp.logical_and(j != 0, j != 1)
    use_next = jnp.logical_and(j != 0, j != nj - 1)
    zeros = jnp.zeros(prev_ref.shape[1:], F32)
    buf_ref[0:CONV_HALO, :] = jnp.where(use_prev, prev_ref[0], zeros)
    buf_ref[CONV_HALO:CONV_HALO + tr, :] = cur_ref[0]
    buf_ref[CONV_HALO + tr:2 * CONV_HALO + tr, :] = jnp.where(use_next, next_ref[0], zeros)
    base = CONV_HALO - CONV_W // 2
    acc = w_ref[0:1, :] * buf_ref[base:base + tr, :]
    for t in range(1, CONV_W):
        acc = acc + w_ref[t:t + 1, :] * buf_ref[base + t:base + t + tr, :]
    if has_bias:
        acc = acc + b_ref[...]
    y = _silu(acc)
    if l2_blocks == 0:
        o_ref[0] = y
    else:
        cblk = pl.program_id(2)

        @pl.when(cblk >= l2_blocks)
        def _():
            o_ref[0] = y

        @pl.when(cblk < l2_blocks)
        def _():
            scale = jnp.where(cblk < l2_scaled_blocks, l2_scale, 1.0).astype(F32)
            for hd in range(y.shape[1] // LANES):
                sl = slice(hd * LANES, (hd + 1) * LANES)
                yh = y[:, sl]
                ss = jnp.sum(yh * yh, axis=-1, keepdims=True)
                o_ref[0, :, sl] = yh * (lax.rsqrt(ss + EPS) * scale)


def _conv_silu(proj, conv_w, conv_b, *, col_off, width, l2_blocks=0, l2_scaled_blocks=0, l2_scale=1.0):
    b, l, _ = proj.shape
    assert col_off % CONV_CW == 0 and width % CONV_CW == 0 and l % ROW_TILE == 0
    coff = col_off // CONV_CW
    hpt = ROW_TILE // CONV_HALO
    n_halo = l // CONV_HALO
    has_bias = conv_b is not None
    args = [proj, proj, proj, conv_w]
    in_specs = [
        pl.BlockSpec((1, ROW_TILE, CONV_CW), lambda bi, j, c: (bi, j, c + coff)),
        pl.BlockSpec((1, CONV_HALO, CONV_CW), lambda bi, j, c: (bi, jnp.maximum(j * hpt - 1, 0), c + coff)),
        pl.BlockSpec((1, CONV_HALO, CONV_CW),
                     lambda bi, j, c: (bi, jnp.minimum((j + 1) * hpt, n_halo - 1), c + coff)),
        pl.BlockSpec((CONV_W, CONV_CW), lambda bi, j, c: (0, c)),
    ]
    if has_bias:
        args.append(conv_b.reshape(1, width))
        in_specs.append(pl.BlockSpec((1, CONV_CW), lambda bi, j, c: (0, c)))
    return pl.pallas_call(
        functools.partial(_conv_kernel, has_bias=has_bias, l2_blocks=l2_blocks,
                          l2_scaled_blocks=l2_scaled_blocks, l2_scale=l2_scale),
        grid=(b, l // ROW_TILE, width // CONV_CW),
        in_specs=in_specs,
        out_specs=pl.BlockSpec((1, ROW_TILE, CONV_CW), lambda bi, j, c: (bi, j, c)),
        out_shape=jax.ShapeDtypeStruct((b, l, width), F32),
        scratch_shapes=[pltpu.VMEM((ROW_TILE + 2 * CONV_HALO, CONV_CW), F32)],
        compiler_params=_cparams(("parallel", "parallel", "parallel")),
        name="conv_silu",
    )(*args)


def _chunk_of(d, t, n_ctx, n_all):
    rev = jnp.where(t < n_ctx, n_ctx - 1 - t, n_all - 1 - (t - n_ctx))
    return jnp.where(d == 0, t, rev)


def _scan_masks(d):
    ii = lax.broadcasted_iota(jnp.int32, (CHUNK, CHUNK), 0)
    jj = lax.broadcasted_iota(jnp.int32, (CHUNK, CHUNK), 1)
    fwd = d == 0
    later = jnp.where(fwd, ii, jj)
    earlier = jnp.where(fwd, jj, ii)
    incl = earlier <= later
    strict = earlier < later
    return ii, jj, fwd, incl, strict


def _even_scan_kernel(xs_ref, bc_ref, dt_ref, rq_ref, rk_ref, rv_ref, cos_ref, sin_ref,
                      dtb_ref, alog_ref, rdec_ref, yssd_ref, yret_ref, hs_ref, hr_ref):
    d = pl.program_id(1)
    t = pl.program_id(2)

    @pl.when(t == 0)
    def _():
        hs_ref[...] = jnp.zeros_like(hs_ref)
        hr_ref[...] = jnp.zeros_like(hr_ref)

    ii, jj, fwd, incl, _ = _scan_masks(d)
    lane = lax.broadcasted_iota(jnp.int32, (CHUNK, LANES), 1)
    row = lax.broadcasted_iota(jnp.int32, (CHUNK, LANES), 0)
    lane_lo = lane < (LANES // 2)
    row_lo = row < (CHUNK // 2)

    dtf = _softplus(dt_ref[0] + dtb_ref[0])
    la = -jnp.exp(alog_ref[0]) * dtf
    tri = jnp.where(incl, 1.0, 0.0).astype(BF16)
    cs = _dot_sel_left(tri, la)
    tot = jnp.sum(la, axis=0, keepdims=True)
    ecs = jnp.exp(cs)
    etot = jnp.broadcast_to(jnp.exp(tot), (CHUNK, LANES))
    cs_t = cs.T
    dt_t = dtf.T
    wend_t = (dtf * jnp.exp(tot - cs)).T

    bc = bc_ref[0]
    n_state = SSD_GROUPS * SSD_STATE
    for g in range(SSD_GROUPS):
        bm = bc[:, g * SSD_STATE:(g + 1) * SSD_STATE]
        cm = bc[:, n_state + g * SSD_STATE:n_state + (g + 1) * SSD_STATE]
        bm_bf = bm.astype(BF16)
        cm_bf = cm.astype(BF16)
        scores = _dot_nt(cm_bf, bm_bf)
        bm_t = bm.T
        heads_per_group = SSD_HEADS // SSD_GROUPS
        for pp in range(heads_per_group // 2):
            pidx = g * (heads_per_group // 2) + pp
            xs_pair = xs_ref[0, :, pidx * LANES:(pidx + 1) * LANES]
            x_lo = jnp.where(lane_lo, xs_pair, 0.0).astype(BF16)
            x_hi = jnp.where(lane_lo, 0.0, xs_pair).astype(BF16)
            h_pair = hs_ref[pidx]
            h_lo = jnp.where(lane_lo, h_pair, 0.0).astype(BF16)
            h_hi = jnp.where(lane_lo, 0.0, h_pair).astype(BF16)
            ms, qs, kws = [], [], []
            for r in (2 * pidx, 2 * pidx + 1):
                dec = jnp.exp(jnp.where(incl, cs[:, r:r + 1] - cs_t[r:r + 1, :], NEG_BIG))
                ms.append((scores * dec * dt_t[r:r + 1, :]).astype(BF16))
                qs.append((cm * ecs[:, r:r + 1]).astype(BF16))
                kws.append((bm_t * wend_t[r:r + 1, :]).astype(BF16))
            lhs = jnp.concatenate(ms + qs, axis=1)
            rhs = jnp.concatenate([x_lo, x_hi, h_lo, h_hi], axis=0)
            yssd_ref[0, 0, :, pidx * LANES:(pidx + 1) * LANES] = _dot(lhs, rhs)
            r0 = 2 * pidx
            et = jnp.where(lane_lo, etot[:, r0:r0 + 1], etot[:, r0 + 1:r0 + 2])
            upd = _dot(jnp.concatenate(kws, axis=1), jnp.concatenate([x_lo, x_hi], axis=0))
            hs_ref[pidx] = h_pair * et + upd

    lg = jax.nn.log_sigmoid(rdec_ref[0])
    dist = jnp.where(fwd, ii - jj, jj - ii).astype(F32)
    cnt = jnp.where(fwd, ii + 1, CHUNK - ii).astype(F32)
    rem_t = jnp.where(fwd, CHUNK - 1 - jj, jj).astype(F32)
    cos = cos_ref[...]
    sin = sin_ref[...]
    half = RET_QK_DIM // 2
    swap_lo = (lane % RET_QK_DIM) < half

    def rotary(tt):
        swapped = jnp.where(swap_lo, pltpu.roll(tt, LANES - half, axis=1), pltpu.roll(tt, half, axis=1))
        return tt * cos + swapped * sin

    for p in range(RET_HEADS // 2):
        sl = slice(p * LANES, (p + 1) * LANES)
        q_pair = rotary(rq_ref[0, :, sl])
        k_pair = rotary(rk_ref[0, :, sl] * (RET_QK_DIM ** -0.5))
        k_pair_bf = k_pair.astype(BF16)
        k_t = k_pair.T
        h_pair = hr_ref[p]
        h_bf = h_pair.astype(BF16)
        kts, vs, es = [], [], []
        for side, hh in enumerate((2 * p, 2 * p + 1)):
            lgb = jnp.broadcast_to(lg[hh:hh + 1, :], (CHUNK, LANES))
            keep = lane_lo if side == 0 else jnp.logical_not(lane_lo)
            keep_rows = row_lo if side == 0 else jnp.logical_not(row_lo)
            q_h = jnp.where(keep, q_pair, 0.0)
            scores = _dot_nt(q_h.astype(BF16), k_pair_bf)
            dec = jnp.where(incl, jnp.exp(lgb * dist), 0.0)
            m = (scores * dec).astype(BF16)
            qe = (q_h * jnp.exp(lgb * cnt)).astype(BF16)
            v_h = rv_ref[0, :, hh * RET_V_DIM:(hh + 1) * RET_V_DIM].astype(BF16)
            y = _dot(jnp.concatenate([m, qe], axis=1), jnp.concatenate([v_h, h_bf], axis=0))
            yret_ref[0, 0, :, hh * RET_V_DIM:(hh + 1) * RET_V_DIM] = y
            kts.append(jnp.where(keep_rows, k_t * jnp.exp(lgb * rem_t), 0.0).astype(BF16))
            vs.append(v_h)
            es.append(jnp.exp(lgb * float(CHUNK)))
        upd = _dot(jnp.concatenate(kts, axis=1), jnp.concatenate(vs, axis=0))
        hr_ref[p] = h_pair * jnp.where(row_lo, es[0], es[1]) + upd


def _even_scan(xbc, dtp, proj, cos_t, sin_t, dt_bias, a_log, ret_decay, *, n_ctx, rq_blk, rk_blk, rv_blk):
    b, l, _ = xbc.shape
    nc = l // CHUNK
    cmap = functools.partial(_chunk_of, n_ctx=n_ctx, n_all=nc)
    out_spec = pl.BlockSpec((1, 1, CHUNK, SSD_INNER), lambda bi, d, t: (bi, d, cmap(d, t), 0))
    return pl.pallas_call(
        _even_scan_kernel,
        grid=(b, 2, nc),
        in_specs=[
            pl.BlockSpec((1, CHUNK, SSD_INNER), lambda bi, d, t: (bi, cmap(d, t), 0)),
            pl.BlockSpec((1, CHUNK, 2 * SSD_GROUPS * SSD_STATE), lambda bi, d, t: (bi, cmap(d, t), 2)),
            pl.BlockSpec((1, CHUNK, LANES), lambda bi, d, t: (bi, cmap(d, t), d)),
            pl.BlockSpec((1, CHUNK, RET_QK), lambda bi, d, t: (bi, cmap(d, t), rq_blk)),
            pl.BlockSpec((1, CHUNK, RET_QK), lambda bi, d, t: (bi, cmap(d, t), rk_blk)),
            pl.BlockSpec((1, CHUNK, RET_V), lambda bi, d, t: (bi, cmap(d, t), rv_blk)),
            pl.BlockSpec((CHUNK, LANES), lambda bi, d, t: (cmap(d, t), 0)),
            pl.BlockSpec((CHUNK, LANES), lambda bi, d, t: (cmap(d, t), 0)),
            pl.BlockSpec((1, 1, LANES), lambda bi, d, t: (d, 0, 0)),
            pl.BlockSpec((1, 1, LANES), lambda bi, d, t: (d, 0, 0)),
            pl.BlockSpec((1, RET_HEADS, LANES), lambda bi, d, t: (d, 0, 0)),
        ],
        out_specs=[out_spec, out_spec],
        out_shape=[jax.ShapeDtypeStruct((b, 2, l, SSD_INNER), F32),
                   jax.ShapeDtypeStruct((b, 2, l, RET_V), F32)],
        scratch_shapes=[pltpu.VMEM((SSD_HEADS // 2, SSD_STATE, LANES), F32),
                        pltpu.VMEM((RET_HEADS // 2, LANES, RET_V_DIM), F32)],
        compiler_params=_cparams(("parallel", "parallel", "arbitrary")),
        name="even_scan",
    )(xbc, xbc, dtp, proj, proj, proj, cos_t, sin_t, dt_bias, a_log, ret_decay)


def _even_out_kernel(ys0_ref, ys1_ref, yr0_ref, yr1_ref, xs_ref, z_ref, rg_ref, dsk_ref, nw_ref, o_ref):
    y = (ys0_ref[0, 0] + ys1_ref[0, 0] + dsk_ref[...] * xs_ref[0]) * _silu(z_ref[0])
    gsz = SSD_INNER // SSD_GROUPS
    for g in range(SSD_GROUPS):
        sl = slice(g * gsz, (g + 1) * gsz)
        o_ref[0, :, sl] = _rms(y[:, sl], nw_ref[:, sl]).astype(BF16)
    yr = yr0_ref[0, 0] + yr1_ref[0, 0]
    gate = _silu(rg_ref[0])
    for hh in range(RET_HEADS):
        sl = slice(hh * RET_V_DIM, (hh + 1) * RET_V_DIM)
        v = yr[:, sl]
        mu = jnp.mean(v, axis=-1, keepdims=True)
        cen = v - mu
        var = jnp.mean(cen * cen, axis=-1, keepdims=True)
        o_ref[0, :, SSD_INNER + hh * RET_V_DIM:SSD_INNER + (hh + 1) * RET_V_DIM] = (
            cen * lax.rsqrt(var + EPS) * gate[:, sl]).astype(BF16)


EVEN_OUT_TILE = 128


def _even_out(y_ssd, y_ret, xbc, proj, d_skip, ssd_norm, *, z_blk, rg_blk):
    b, _, l, _ = y_ssd.shape
    tr = EVEN_OUT_TILE
    w = SSD_INNER
    dsk = jnp.repeat(d_skip, SSD_HEAD_DIM).reshape(1, w)

    def dir_spec(dd):
        return pl.BlockSpec((1, 1, tr, w), lambda bi, j: (bi, dd, j, 0))

    def col_spec(blk):
        return pl.BlockSpec((1, tr, w), lambda bi, j: (bi, j, blk))

    vec = pl.BlockSpec((1, w), lambda bi, j: (0, 0))
    return pl.pallas_call(
        _even_out_kernel,
        grid=(b, l // tr),
        in_specs=[dir_spec(0), dir_spec(1), dir_spec(0), dir_spec(1), col_spec(0), col_spec(z_blk),
                  col_spec(rg_blk), vec, vec],
        out_specs=pl.BlockSpec((1, tr, 2 * w), lambda bi, j: (bi, j, 0)),
        out_shape=jax.ShapeDtypeStruct((b, l, 2 * w), BF16),
        compiler_params=_cparams(("parallel", "parallel")),
        name="even_out",
    )(y_ssd, y_ssd, y_ret, y_ret, xbc, proj, proj, dsk, ssd_norm.reshape(1, w))


GDN_LEVELS = 7


def _gdn_scan_kernel(q_ref, k_ref, v_ref, raw_ref, bias_ref, alog_ref, lvl_ref, o_ref,
                     s_ref, cst_ref, wendt_ref, col_ref):
    d = pl.program_id(1)
    t = pl.program_id(2)

    @pl.when(t == 0)
    def _():
        s_ref[...] = jnp.zeros_like(s_ref)

    ii, jj, fwd, incl, strict = _scan_masks(d)
    strict_f = jnp.where(strict, 1.0, 0.0)
    eye = jnp.where(ii == jj, 1.0, 0.0)
    tri = jnp.where(incl, 1.0, 0.0).astype(BF16)

    raw = raw_ref[0]
    beta = jax.nn.sigmoid(raw)
    gl = -jnp.exp(alog_ref[0]) * _softplus(raw + bias_ref[0])
    cs = _dot_sel_left(tri, gl)
    tot = jnp.sum(gl, axis=0, keepdims=True)
    cst_ref[...] = cs.T
    wendt_ref[...] = jnp.exp(tot - cs).T
    col_ref[0:CHUNK, :] = cs
    col_ref[CHUNK:2 * CHUNK, :] = beta
    col_ref[2 * CHUNK:3 * CHUNK, :] = jnp.broadcast_to(tot, (CHUNK, LANES))
    sel_row = lax.broadcasted_iota(jnp.int32, (LANES, LANES), 0)

    rep = GDN_V_HEADS // GDN_K_HEADS

    def k_head(kh, carry):
        off = pl.multiple_of(kh * GDN_HEAD_DIM, GDN_HEAD_DIM)
        q = q_ref[0, :, pl.ds(off, GDN_HEAD_DIM)]
        k = k_ref[0, :, pl.ds(off, GDN_HEAD_DIM)]
        q_bf = q.astype(BF16)
        k_bf = k.astype(BF16)
        kk = _dot_nt(k_bf, k_bf)
        qk = _dot_nt(q_bf, k_bf)
        k_t = k.T
        for sub in range(rep):
            h = kh * rep + sub
            voff = pl.multiple_of(h * GDN_HEAD_DIM, GDN_HEAD_DIM)
            v = v_ref[0, :, pl.ds(voff, GDN_HEAD_DIM)]
            sel_g = jnp.where(sel_row == h + GDN_V_HEADS, 1.0, 0.0).astype(BF16)
            sel_b = jnp.where(sel_row == h, 1.0, 0.0).astype(BF16)
            cs_b = _dot_sel_right(col_ref[0:CHUNK, :], sel_g)
            beta_b = _dot_sel_right(col_ref[CHUNK:2 * CHUNK, :], sel_b)
            tot_b = _dot_sel_right(col_ref[2 * CHUNK:3 * CHUNK, :], sel_g)
            cs_r = cst_ref[pl.ds(h + GDN_V_HEADS, 1), :]
            gam = jnp.exp(jnp.where(incl, cs_b - cs_r, NEG_BIG))
            ecs_b = jnp.exp(cs_b)
            low = kk * gam * strict_f * beta_b
            tinv = eye
            for lv in range(GDN_LEVELS):
                a_off = (low * lvl_ref[lv]).astype(BF16)
                x = _dot(a_off, tinv.astype(BF16))
                tinv = tinv - _dot(tinv.astype(BF16), x.astype(BF16))
            rhs = jnp.concatenate([(v * beta_b).astype(BF16), (k * (beta_b * ecs_b)).astype(BF16)], axis=1)
            sol = _dot(tinv.astype(BF16), rhs)
            u = sol[:, :GDN_HEAD_DIM]
            w = sol[:, GDN_HEAD_DIM:]
            state = s_ref[h]
            wq = jnp.concatenate([w.astype(BF16), (q * ecs_b).astype(BF16)], axis=0)
            ws_qs = _dot(wq, state.astype(BF16))
            v_new = u - ws_qs[:CHUNK]
            v_new_bf = v_new.astype(BF16)
            att = (qk * gam).astype(BF16)
            o_ref[0, 0, :, pl.ds(voff, GDN_HEAD_DIM)] = ws_qs[CHUNK:] + _dot(att, v_new_bf)
            k_end_t = (k_t * wendt_ref[pl.ds(h + GDN_V_HEADS, 1), :]).astype(BF16)
            s_ref[h] = state * jnp.exp(tot_b) + _dot(k_end_t, v_new_bf)
        return carry

    lax.fori_loop(0, GDN_K_HEADS, k_head, 0)


def _gdn_level_masks():
    i = jnp.arange(CHUNK)[:, None]
    j = jnp.arange(CHUNK)[None, :]
    masks = []
    for lv in range(GDN_LEVELS):
        blk = 1 << lv
        masks.append(((i // (2 * blk) == j // (2 * blk)) & (i // blk != j // blk)).astype(F32))
    return jnp.stack(masks)


def _gdn_scan(qkv, raw, bias, a_log, *, n_ctx):
    b, l, _ = qkv.shape
    nc = l // CHUNK
    cmap = functools.partial(_chunk_of, n_ctx=n_ctx, n_all=nc)
    return pl.pallas_call(
        _gdn_scan_kernel,
        grid=(b, 2, nc),
        in_specs=[
            pl.BlockSpec((1, CHUNK, GDN_K), lambda bi, d, t: (bi, cmap(d, t), 0)),
            pl.BlockSpec((1, CHUNK, GDN_K), lambda bi, d, t: (bi, cmap(d, t), 1)),
            pl.BlockSpec((1, CHUNK, GDN_V), lambda bi, d, t: (bi, cmap(d, t), 1)),
            pl.BlockSpec((1, CHUNK, LANES), lambda bi, d, t: (bi, cmap(d, t), d)),
            pl.BlockSpec((1, 1, LANES), lambda bi, d, t: (d, 0, 0)),
            pl.BlockSpec((1, 1, LANES), lambda bi, d, t: (d, 0, 0)),
            pl.BlockSpec((GDN_LEVELS, CHUNK, CHUNK), lambda bi, d, t: (0, 0, 0)),
        ],
        out_specs=pl.BlockSpec((1, 1, CHUNK, GDN_V), lambda bi, d, t: (bi, d, cmap(d, t), 0)),
        out_shape=jax.ShapeDtypeStruct((b, 2, l, GDN_V), F32),
        scratch_shapes=[pltpu.VMEM((GDN_V_HEADS, GDN_HEAD_DIM, GDN_HEAD_DIM), F32),
                        pltpu.VMEM((LANES, CHUNK), F32),
                        pltpu.VMEM((LANES, CHUNK), F32),
                        pltpu.VMEM((3 * CHUNK, LANES), F32)],
        compiler_params=_cparams(("parallel", "parallel", "arbitrary")),
        name="gdn_scan",
    )(qkv, qkv, qkv, raw, bias, a_log, _gdn_level_masks())


def _odd_out_kernel(o0_ref, o1_ref, z_ref, nw_ref, out_ref):
    o = o0_ref[0, 0] + o1_ref[0, 0]
    gate = _silu(z_ref[0])
    for hh in range(GDN_V_HEADS):
        sl = slice(hh * GDN_HEAD_DIM, (hh + 1) * GDN_HEAD_DIM)
        out_ref[0, :, sl] = (_rms(o[:, sl], nw_ref[...]) * gate[:, sl]).astype(BF16)


def _odd_out(o, proj, norm_w, *, z_blk):
    b, _, l, w = o.shape
    tr = ROW_TILE

    def dir_spec(dd):
        return pl.BlockSpec((1, 1, tr, w), lambda bi, j: (bi, dd, j, 0))

    return pl.pallas_call(
        _odd_out_kernel,
        grid=(b, l // tr),
        in_specs=[dir_spec(0), dir_spec(1),
                  pl.BlockSpec((1, tr, w), lambda bi, j: (bi, j, z_blk)),
                  pl.BlockSpec((1, GDN_HEAD_DIM), lambda bi, j: (0, 0))],
        out_specs=pl.BlockSpec((1, tr, w), lambda bi, j: (bi, j, 0)),
        out_shape=jax.ShapeDtypeStruct((b, l, w), BF16),
        compiler_params=_cparams(("parallel", "parallel")),
        name="odd_out",
    )(o, o, proj, norm_w.reshape(1, GDN_HEAD_DIM))


def _dir_lanes(p, lane_off):
    out = jnp.zeros((2, 1, LANES), F32)
    return out.at[:, 0, lane_off:lane_off + p.shape[1]].set(p.astype(F32))


def _rotary_tables(n_ctx_rows, n_lat_rows):
    nf = RET_QK_DIM // 4
    pos = jnp.arange(n_lat_rows)
    rowp = (pos // GRID_W).astype(F32)
    colp = (pos % GRID_W).astype(F32)
    freqs = ROPE_BASE ** (-jnp.arange(nf, dtype=F32) / nf)
    ang = jnp.concatenate([rowp[:, None] * freqs, colp[:, None] * freqs], axis=-1)
    cos = jnp.cos(ang)
    sin = jnp.sin(ang)
    cos_h = jnp.concatenate([cos, cos], axis=-1)
    sin_h = jnp.concatenate([-sin, sin], axis=-1)
    reps = LANES // RET_QK_DIM
    cos_t = jnp.concatenate([jnp.ones((n_ctx_rows, LANES), F32), jnp.tile(cos_h, (1, reps))], axis=0)
    sin_t = jnp.concatenate([jnp.zeros((n_ctx_rows, LANES), F32), jnp.tile(sin_h, (1, reps))], axis=0)
    return cos_t, sin_t


EV_Z, EV_RG, EV_RV, EV_XBC, EV_RQ, EV_RK = 0, 2048, 4096, 6144, 9216, 10240
EV_MAIN = 11264
SSD_CONV_DIM = SSD_INNER + 2 * SSD_GROUPS * SSD_STATE


def _even_weights(w_in):
    o = 0
    parts = {}
    for name, sz in (("z", SSD_INNER), ("xbc", SSD_CONV_DIM), ("dt", 2 * SSD_HEADS), ("rq", RET_QK),
                     ("rk", RET_QK), ("rv", RET_V), ("rg", RET_V)):
        parts[name] = w_in[:, o:o + sz]
        o += sz
    main = jnp.concatenate([parts[nm] for nm in ("z", "rg", "rv", "xbc", "rq", "rk")], axis=1).astype(BF16)
    pad = jnp.zeros((w_in.shape[0], LANES - SSD_HEADS), w_in.dtype)
    dt = parts["dt"]
    w_dt = jnp.concatenate([dt[:, :SSD_HEADS], pad, dt[:, SSD_HEADS:], pad], axis=1)
    return main[None], w_dt[None]


def _odd_small_weights(w_in):
    base = 2 * GDN_K + 2 * GDN_V
    hv = GDN_V_HEADS
    beta = w_in[:, base:base + 2 * hv]
    dec = w_in[:, base + 2 * hv:base + 4 * hv]
    pad = jnp.zeros((w_in.shape[0], LANES - 2 * hv), w_in.dtype)
    return jnp.concatenate([beta[:, :hv], dec[:, :hv], pad, beta[:, hv:], dec[:, hv:], pad], axis=1)[None]


def _ffn(s, o, mod, layer, post_mix, pre_ffn, w1, w3, w2):
    b, l, d = s.shape
    s1, h = _res_norm(s, mod, o=o, gate=(layer, 2), post_w=post_mix, pre_w=pre_ffn,
                      shift=(layer, 3), scale=(layer, 4))
    d_ff = w1.shape[2]
    act = _matmul(h.reshape(b * l, d), [w1, w3], layer=layer, n=d_ff, tn=256, out_dtype=BF16)
    f = _matmul_ktiled(act, w2, layer=layer)
    return s1, f.reshape(b, l, d)


def kernel(x, c, ctx, c_ctx, ada_w, ada_b, norm_mix_pre, norm_mix_post, norm_ffn_pre, norm_ffn_post,
           ev_w_in, ev_conv_w, ev_conv_b, ev_dt_bias, ev_a_log, ev_d_skip, ev_ssd_norm, ev_ret_decay,
           ev_w_out, od_w_in, od_conv_w, od_dt_bias, od_a_log, od_norm, od_w_out,
           ffn_w1, ffn_w3, ffn_w2):
    b, seq, d = x.shape
    n_ctx_rows = ctx.shape[1]
    assert n_ctx_rows == ROW_TILE and seq % ROW_TILE == 0
    l = n_ctx_rows + seq
    m = b * l
    n_ctx = n_ctx_rows // CHUNK

    s = jnp.concatenate([ctx, x], axis=1)
    mod = _ada(jnp.concatenate([c, c_ctx[None]], axis=0), ada_w, ada_b)

    _, h = _res_norm(s, mod, pre_w=norm_mix_pre[0], shift=(0, 0), scale=(0, 1))
    w_main, w_dt = _even_weights(ev_w_in[0])
    h2 = h.reshape(m, d)
    proj = _matmul(h2, [w_main], n=EV_MAIN, tn=512, out_dtype=F32).reshape(b, l, EV_MAIN)
    dtp = _matmul(h2, [w_dt], n=2 * LANES, tn=2 * LANES, out_dtype=F32).reshape(b, l, 2 * LANES)
    xbc = _conv_silu(proj, ev_conv_w[0], ev_conv_b[0], col_off=EV_XBC, width=SSD_CONV_DIM)
    cos_t, sin_t = _rotary_tables(n_ctx_rows, seq)
    rdec = jnp.broadcast_to(ev_ret_decay[0][:, :, None], (2, RET_HEADS, LANES)).astype(F32)
    y_ssd, y_ret = _even_scan(xbc, dtp, proj, cos_t, sin_t, _dir_lanes(ev_dt_bias[0], 0),
                              _dir_lanes(ev_a_log[0], 0), rdec, n_ctx=n_ctx,
                              rq_blk=EV_RQ // RET_QK, rk_blk=EV_RK // RET_QK, rv_blk=EV_RV // RET_V)
    act = _even_out(y_ssd, y_ret, xbc, proj, ev_d_skip[0], ev_ssd_norm[0],
                    z_blk=EV_Z // SSD_INNER, rg_blk=EV_RG // SSD_INNER)
    o = _matmul(act.reshape(m, act.shape[-1]),[ev_w_out], layer=0, n=d, tn=512, out_dtype=F32).reshape(b, l, d)
    s, f = _ffn(s, o, mod, 0, norm_mix_post[0], norm_ffn_pre[0], ffn_w1, ffn_w3, ffn_w2)

    s, h = _res_norm(s, mod, o=f, gate=(0, 5), post_w=norm_ffn_post[0], pre_w=norm_mix_pre[1],
                     shift=(1, 0), scale=(1, 1))
    h2 = h.reshape(m, d)
    n_main = 2 * GDN_K + 2 * GDN_V
    proj = _matmul(h2, [od_w_in], layer=0, n=n_main, tn=512, out_dtype=F32).reshape(b, l, n_main)
    raw = _matmul(h2, [_odd_small_weights(od_w_in[0])], n=2 * LANES, tn=2 * LANES,
                  out_dtype=F32).reshape(b, l, 2 * LANES)
    qkv = _conv_silu(proj, od_conv_w[0], None, col_off=0, width=2 * GDN_K + GDN_V,
                     l2_blocks=2 * GDN_K // CONV_CW, l2_scaled_blocks=GDN_K // CONV_CW,
                     l2_scale=GDN_HEAD_DIM ** -0.5)
    o2 = _gdn_scan(qkv, raw, _dir_lanes(od_dt_bias[0], GDN_V_HEADS), _dir_lanes(od_a_log[0], GDN_V_HEADS),
                   n_ctx=n_ctx)
    act = _odd_out(o2, proj, od_norm[0], z_blk=(2 * GDN_K + GDN_V) // GDN_V)
    o = _matmul(act.reshape(m, act.shape[-1]),[od_w_out], layer=0, n=d, tn=512, out_dtype=F32).reshape(b, l, d)
    s, f = _ffn(s, o, mod, 1, norm_mix_post[1], norm_ffn_pre[1], ffn_w1, ffn_w3, ffn_w2)

    out, _ = _res_norm(s, mod, o=f, gate=(1, 5), post_w=norm_ffn_post[1], row_off=1, out_rows=seq)
    return out
```

```python
import functools
import math

import jax
import jax.numpy as jnp
from jax import lax
from jax.experimental import pallas as pl
from jax.experimental.pallas import tpu as pltpu

F32 = jnp.float32
BF16 = jnp.bfloat16

EPS = 1e-6
CHUNK = 128
LANES = 128
CONV_W = 5
CONV_HALO = 16
GRID_W = 64
ROPE_BASE = 10000.0
ROW_TILE = 256
NEG_BIG = -1e30

SSD_HEADS = 32
SSD_HEAD_DIM = 64
SSD_GROUPS = 4
SSD_STATE = 128
SSD_INNER = SSD_HEADS * SSD_HEAD_DIM
RET_HEADS = 16
RET_QK_DIM = 64
RET_V_DIM = 128
RET_QK = RET_HEADS * RET_QK_DIM
RET_V = RET_HEADS * RET_V_DIM
GDN_K_HEADS = 16
GDN_V_HEADS = 32
GDN_HEAD_DIM = 128
GDN_K = GDN_K_HEADS * GDN_HEAD_DIM
GDN_V = GDN_V_HEADS * GDN_HEAD_DIM

VMEM_LIMIT = 56 * 1024 * 1024


def _cparams(sem):
    return pltpu.CompilerParams(dimension_semantics=sem, vmem_limit_bytes=VMEM_LIMIT)


def _dot(a, b):
    return jnp.dot(a, b, preferred_element_type=F32)


def _dot_nt(a, b):
    return lax.dot_general(a, b, (((1,), (1,)), ((), ())), preferred_element_type=F32)


def _split3(x):
    hi = x.astype(BF16)
    r1 = x - hi.astype(F32)
    mid = r1.astype(BF16)
    lo = (r1 - mid.astype(F32)).astype(BF16)
    return hi, mid, lo


def _dot_sel_left(sel, x):
    hi, mid, lo = _split3(x)
    return _dot(sel, hi) + _dot(sel, mid) + _dot(sel, lo)


def _dot_sel_right(x, sel):
    hi, mid, lo = _split3(x)
    return _dot(hi, sel) + _dot(mid, sel) + _dot(lo, sel)


def _silu(x):
    return x * jax.nn.sigmoid(x)


def _softplus(x):
    return jnp.maximum(x, 0.0) + jnp.log(1.0 + jnp.exp(-jnp.abs(x)))


def _rms(x, w):
    ms = jnp.mean(x * x, axis=-1, keepdims=True)
    return x * lax.rsqrt(ms + EPS) * w


ADA_TN = 512


def _ada_kernel(cb_ref, w_ref, b_ref, o_ref, act_ref, *, n_rows):
    first = jnp.logical_and(pl.program_id(0) == 0, pl.program_id(1) == 0)

    @pl.when(first)
    def _():
        act_ref[...] = _silu(cb_ref[...])

    d = w_ref.shape[1]
    tn = w_ref.shape[2]
    w3 = w_ref[0].reshape(d // 8, 8, tn)
    o_ref[...] = jnp.zeros_like(o_ref)
    for r in range(n_rows):
        a3 = act_ref[r].reshape(d // 8, 8, LANES)
        for lt in range(tn // LANES):
            sl = slice(lt * LANES, (lt + 1) * LANES)
            part = jnp.sum(w3[:, :, sl] * a3, axis=0)
            o_ref[0, r:r + 1, sl] = jnp.sum(part, axis=0, keepdims=True) + b_ref[0, :, sl]


def _ada(cvec, ada_w, ada_b):
    n_rows, d = cvec.shape
    depth, _, n = ada_w.shape
    cb = jnp.broadcast_to(cvec[:, :, None], (n_rows, d, LANES))
    return pl.pallas_call(
        functools.partial(_ada_kernel, n_rows=n_rows),
        grid=(depth, n // ADA_TN),
        in_specs=[
            pl.BlockSpec((n_rows, d, LANES), lambda i, j: (0, 0, 0)),
            pl.BlockSpec((1, d, ADA_TN), lambda i, j: (i, 0, j)),
            pl.BlockSpec((1, 1, ADA_TN), lambda i, j: (i, 0, j)),
        ],
        out_specs=pl.BlockSpec((1, 8, ADA_TN), lambda i, j: (i, 0, j)),
        out_shape=jax.ShapeDtypeStruct((depth, 8, n), F32),
        scratch_shapes=[pltpu.VMEM((n_rows, d, LANES), F32)],
        compiler_params=_cparams(("arbitrary", "arbitrary")),
        name="ada_mod",
    )(cb, ada_w, ada_b.reshape(depth, 1, n))


def _mod_row(m_ref, ctx_row):
    r = pl.program_id(0)
    if ctx_row is not None:
        r = jnp.where(pl.program_id(1) == 0, ctx_row, r)
    return m_ref[0, pl.ds(r, 1), :]


def _res_norm_kernel(*refs, has_res, has_h, ctx_row):
    it = iter(refs)
    s_ref = next(it)
    if has_res:
        o_ref, gate_ref, post_ref = next(it), next(it), next(it)
    if has_h:
        pre_ref, shift_ref, scale_ref = next(it), next(it), next(it)
    s = s_ref[0]
    if has_res:
        s = s + _mod_row(gate_ref, ctx_row) * _rms(o_ref[0].astype(F32), post_ref[...])
        next(it)[0] = s
    if has_h:
        h = _rms(s, pre_ref[...]) * (1.0 + _mod_row(scale_ref, ctx_row)) + _mod_row(shift_ref, ctx_row)
        next(it)[0] = h.astype(BF16)


def _res_norm(s, mod, *, o=None, gate=None, post_w=None, pre_w=None, shift=None, scale=None,
              row_off=0, out_rows=None):
    b, l, d = s.shape
    has_res = o is not None
    has_h = pre_w is not None
    n_tiles = (l // ROW_TILE - row_off) if out_rows is None else out_rows // ROW_TILE
    ctx_row = None if row_off else b

    def rows(bi, j):
        return (bi, j + row_off, 0)

    def outrows(bi, j):
        return (bi, j, 0)

    def mod_spec(sel):
        layer, comp = sel
        return pl.BlockSpec((1, 8, d), lambda bi, j: (layer, 0, comp))

    vec_spec = pl.BlockSpec((1, d), lambda bi, j: (0, 0))
    tile = (1, ROW_TILE, d)
    args, in_specs, out_specs, out_shape = [s], [pl.BlockSpec(tile, rows)], [], []
    if has_res:
        args += [o, mod, post_w.reshape(1, d)]
        in_specs += [pl.BlockSpec(tile, rows), mod_spec(gate), vec_spec]
        out_specs.append(pl.BlockSpec(tile, outrows))
        out_shape.append(jax.ShapeDtypeStruct((b, n_tiles * ROW_TILE, d), F32))
    if has_h:
        args += [pre_w.reshape(1, d), mod, mod]
        in_specs += [vec_spec, mod_spec(shift), mod_spec(scale)]
        out_specs.append(pl.BlockSpec(tile, outrows))
        out_shape.append(jax.ShapeDtypeStruct((b, n_tiles * ROW_TILE, d), BF16))
    outs = pl.pallas_call(
        functools.partial(_res_norm_kernel, has_res=has_res, has_h=has_h, ctx_row=ctx_row),
        grid=(b, n_tiles),
        in_specs=in_specs,
        out_specs=out_specs,
        out_shape=out_shape,
        compiler_params=_cparams(("parallel", "parallel")),
        name="res_norm",
    )(*args)
    outs = list(outs)
    s_new = outs.pop(0) if has_res else None
    h = outs.pop(0) if has_h else None
    return s_new, h


MM_TM = 1536


def _row_tile(m, cap=MM_TM):
    best = ROW_TILE
    for tm in range(ROW_TILE, cap + 1, ROW_TILE):
        if m % tm == 0:
            best = tm
    return best


def _mm_kernel(a_ref, *refs, swiglu):
    a = a_ref[...]
    if swiglu:
        w1_ref, w3_ref, o_ref = refs
        g = _dot(a, w1_ref[0].astype(BF16))
        u = _dot(a, w3_ref[0].astype(BF16))
        o_ref[...] = (_silu(g) * u).astype(o_ref.dtype)
    else:
        w_ref, o_ref = refs
        o_ref[...] = _dot(a, w_ref[0].astype(BF16)).astype(o_ref.dtype)


def _matmul(a, ws, *, layer=0, col_off=0, n, tn, out_dtype):
    m, k = a.shape
    swiglu = len(ws) == 2
    tm = _row_tile(m)
    assert col_off % tn == 0
    off = col_off // tn
    w_spec = pl.BlockSpec((1, k, tn), lambda i, j: (layer, 0, j + off))
    return pl.pallas_call(
        functools.partial(_mm_kernel, swiglu=swiglu),
        grid=(m // tm, pl.cdiv(n, tn)),
        in_specs=[pl.BlockSpec((tm, k), lambda i, j: (i, 0), pipeline_mode=pl.Buffered(1))]
        + [w_spec] * len(ws),
        out_specs=pl.BlockSpec((tm, tn), lambda i, j: (i, j)),
        out_shape=jax.ShapeDtypeStruct((m, n), out_dtype),
        compiler_params=_cparams(("parallel", "arbitrary")),
        name="matmul_swiglu" if swiglu else "matmul",
    )(a, *ws)


MMK_TN = 1024
MMK_TK = 1024


def _mm_ktiled_kernel(a_ref, w_ref, o_ref, acc_ref, *, k_total, tk):
    kk = pl.program_id(2)
    nk = pl.num_programs(2)

    @pl.when(kk == 0)
    def _():
        acc_ref[...] = jnp.zeros_like(acc_ref)

    rem = k_total - (pl.cdiv(k_total, tk) - 1) * tk

    def step(a, w):
        acc_ref[...] += _dot(a, w.astype(BF16))

    if rem == tk:
        step(a_ref[...], w_ref[0])
    else:
        @pl.when(kk < nk - 1)
        def _():
            step(a_ref[...], w_ref[0])

        @pl.when(kk == nk - 1)
        def _():
            a = a_ref[...]
            w = w_ref[0]
            ka = lax.broadcasted_iota(jnp.int32, a.shape, 1)
            kw = lax.broadcasted_iota(jnp.int32, w.shape, 0)
            step(jnp.where(ka < rem, a, jnp.zeros_like(a)), jnp.where(kw < rem, w, jnp.zeros_like(w)))

    @pl.when(kk == nk - 1)
    def _():
        o_ref[...] = acc_ref[...].astype(o_ref.dtype)


def _matmul_ktiled(a, w, *, layer, tk=MMK_TK):
    m, k = a.shape
    n = w.shape[2]
    tm = _row_tile(m)
    tn = min(MMK_TN, n)
    assert n % tn == 0
    return pl.pallas_call(
        functools.partial(_mm_ktiled_kernel, k_total=k, tk=tk),
        grid=(m // tm, n // tn, pl.cdiv(k, tk)),
        in_specs=[pl.BlockSpec((tm, tk), lambda i, j, kk: (i, kk)),
                  pl.BlockSpec((1, tk, tn), lambda i, j, kk: (layer, kk, j))],
        out_specs=pl.BlockSpec((tm, tn), lambda i, j, kk: (i, j)),
        out_shape=jax.ShapeDtypeStruct((m, n), BF16),
        scratch_shapes=[pltpu.VMEM((tm, tn), F32)],
        compiler_params=_cparams(("parallel", "parallel", "arbitrary")),
        name="matmul_ktiled",
    )(a, w)


CONV_CW = 1024


def _conv_kernel(cur_ref, prev_ref, next_ref, w_ref, *refs, has_bias, l2_blocks, l2_scaled_blocks,
                 l2_scale):
    if has_bias:
        b_ref, o_ref, buf_ref = refs
    else:
        o_ref, buf_ref = refs
    j = pl.program_id(1)
    nj = pl.num_programs(1)
    tr = cur_ref.shape[1]
    use_prev = jnp.logical_and(j != 0, j != 1)
    use_next = jnp.logical_and(j != 0, j != nj - 1)
    zeros = jnp.zeros(prev_ref.shape[1:], F32)
    buf_ref[0:CONV_HALO, :] = jnp.where(use_prev, prev_ref[0].astype(F32), zeros)
    buf_ref[CONV_HALO:CONV_HALO + tr, :] = cur_ref[0].astype(F32)
    buf_ref[CONV_HALO + tr:2 * CONV_HALO + tr, :] = jnp.where(use_next, next_ref[0].astype(F32), zeros)
    base = CONV_HALO - CONV_W // 2
    acc = w_ref[0:1, :] * buf_ref[base:base + tr, :]
    for t in range(1, CONV_W):
        acc = acc + w_ref[t:t + 1, :] * buf_ref[base + t:base + t + tr, :]
    if has_bias:
        acc = acc + b_ref[...]
    y = _silu(acc)
    if l2_blocks == 0:
        o_ref[0] = y.astype(o_ref.dtype)
    else:
        cblk = pl.program_id(2)

        @pl.when(cblk >= l2_blocks)
        def _():
            o_ref[0] = y.astype(o_ref.dtype)

        @pl.when(cblk < l2_blocks)
        def _():
            scale = jnp.where(cblk < l2_scaled_blocks, l2_scale, 1.0).astype(F32)
            for hd in range(y.shape[1] // LANES):
                sl = slice(hd * LANES, (hd + 1) * LANES)
                yh = y[:, sl]
                ss = jnp.sum(yh * yh, axis=-1, keepdims=True)
                o_ref[0, :, sl] = (yh * (lax.rsqrt(ss + EPS) * scale)).astype(o_ref.dtype)


def _conv_silu(proj, conv_w, conv_b, *, col_off, width, l2_blocks=0, l2_scaled_blocks=0, l2_scale=1.0):
    b, l, _ = proj.shape
    assert col_off % CONV_CW == 0 and width % CONV_CW == 0 and l % ROW_TILE == 0
    coff = col_off // CONV_CW
    hpt = ROW_TILE // CONV_HALO
    n_halo = l // CONV_HALO
    has_bias = conv_b is not None
    args = [proj, proj, proj, conv_w]
    in_specs = [
        pl.BlockSpec((1, ROW_TILE, CONV_CW), lambda bi, j, c: (bi, j, c + coff)),
        pl.BlockSpec((1, CONV_HALO, CONV_CW), lambda bi, j, c: (bi, jnp.maximum(j * hpt - 1, 0), c + coff)),
        pl.BlockSpec((1, CONV_HALO, CONV_CW),
                     lambda bi, j, c: (bi, jnp.minimum((j + 1) * hpt, n_halo - 1), c + coff)),
        pl.BlockSpec((CONV_W, CONV_CW), lambda bi, j, c: (0, c)),
    ]
    if has_bias:
        args.append(conv_b.reshape(1, width))
        in_specs.append(pl.BlockSpec((1, CONV_CW), lambda bi, j, c: (0, c)))
    return pl.pallas_call(
        functools.partial(_conv_kernel, has_bias=has_bias, l2_blocks=l2_blocks,
                          l2_scaled_blocks=l2_scaled_blocks, l2_scale=l2_scale),
        grid=(b, l // ROW_TILE, width // CONV_CW),
        in_specs=in_specs,
        out_specs=pl.BlockSpec((1, ROW_TILE, CONV_CW), lambda bi, j, c: (bi, j, c)),
        out_shape=jax.ShapeDtypeStruct((b, l, width), BF16),
        scratch_shapes=[pltpu.VMEM((ROW_TILE + 2 * CONV_HALO, CONV_CW), F32)],
        compiler_params=_cparams(("parallel", "parallel", "parallel")),
        name="conv_silu",
    )(*args)


def _chunk_of(d, t, n_ctx, n_all):
    rev = jnp.where(t < n_ctx, n_ctx - 1 - t, n_all - 1 - (t - n_ctx))
    return jnp.where(d == 0, t, rev)


def _scan_masks(d):
    ii = lax.broadcasted_iota(jnp.int32, (CHUNK, CHUNK), 0)
    jj = lax.broadcasted_iota(jnp.int32, (CHUNK, CHUNK), 1)
    fwd = d == 0
    later = jnp.where(fwd, ii, jj)
    earlier = jnp.where(fwd, jj, ii)
    incl = earlier <= later
    strict = earlier < later
    return ii, jj, fwd, incl, strict


def _even_scan_kernel(xs_ref, bc_ref, dt_ref, rq_ref, rk_ref, rv_ref, cos_ref, sin_ref,
                      dtb_ref, alog_ref, rdec_ref, yssd_ref, yret_ref, hs_ref, hr_ref):
    d = pl.program_id(1)
    t = pl.program_id(2)

    @pl.when(t == 0)
    def _():
        hs_ref[...] = jnp.zeros_like(hs_ref)
        hr_ref[...] = jnp.zeros_like(hr_ref)

    ii, jj, fwd, incl, _ = _scan_masks(d)
    lane = lax.broadcasted_iota(jnp.int32, (CHUNK, LANES), 1)
    row = lax.broadcasted_iota(jnp.int32, (CHUNK, LANES), 0)
    lane_lo = lane < (LANES // 2)
    row_lo = row < (CHUNK // 2)

    dtf = _softplus(dt_ref[0] + dtb_ref[0])
    la = -jnp.exp(alog_ref[0]) * dtf
    tri = jnp.where(incl, 1.0, 0.0).astype(BF16)
    cs = _dot_sel_left(tri, la)
    tot = jnp.sum(la, axis=0, keepdims=True)
    ecs = jnp.exp(cs)
    etot = jnp.broadcast_to(jnp.exp(tot), (CHUNK, LANES))
    cs_t = cs.T
    dt_t = dtf.T
    wend_t = (dtf * jnp.exp(tot - cs)).T

    bc = bc_ref[0].astype(F32)
    n_state = SSD_GROUPS * SSD_STATE
    for g in range(SSD_GROUPS):
        bm = bc[:, g * SSD_STATE:(g + 1) * SSD_STATE]
        cm = bc[:, n_state + g * SSD_STATE:n_state + (g + 1) * SSD_STATE]
        bm_bf = bm.astype(BF16)
        cm_bf = cm.astype(BF16)
        scores = _dot_nt(cm_bf, bm_bf)
        bm_t = bm.T
        heads_per_group = SSD_HEADS // SSD_GROUPS
        for pp in range(heads_per_group // 2):
            pidx = g * (heads_per_group // 2) + pp
            xs_pair = xs_ref[0, :, pidx * LANES:(pidx + 1) * LANES]
            x_lo = jnp.where(lane_lo, xs_pair, 0.0).astype(BF16)
            x_hi = jnp.where(lane_lo, 0.0, xs_pair).astype(BF16)
            h_pair = hs_ref[pidx]
            h_lo = jnp.where(lane_lo, h_pair, 0.0).astype(BF16)
            h_hi = jnp.where(lane_lo, 0.0, h_pair).astype(BF16)
            ms, qs, kws = [], [], []
            for r in (2 * pidx, 2 * pidx + 1):
                dec = jnp.exp(jnp.where(incl, cs[:, r:r + 1] - cs_t[r:r + 1, :], NEG_BIG))
                ms.append((scores * dec * dt_t[r:r + 1, :]).astype(BF16))
                qs.append((cm * ecs[:, r:r + 1]).astype(BF16))
                kws.append((bm_t * wend_t[r:r + 1, :]).astype(BF16))
            lhs = jnp.concatenate(ms + qs, axis=1)
            rhs = jnp.concatenate([x_lo, x_hi, h_lo, h_hi], axis=0)
            yssd_ref[0, 0, :, pidx * LANES:(pidx + 1) * LANES] = _dot(lhs, rhs).astype(yssd_ref.dtype)
            r0 = 2 * pidx
            et = jnp.where(lane_lo, etot[:, r0:r0 + 1], etot[:, r0 + 1:r0 + 2])
            upd = _dot(jnp.concatenate(kws, axis=1), jnp.concatenate([x_lo, x_hi], axis=0))
            hs_ref[pidx] = h_pair * et + upd

    lg = jax.nn.log_sigmoid(rdec_ref[0])
    dist = jnp.where(fwd, ii - jj, jj - ii).astype(F32)
    cnt = jnp.where(fwd, ii + 1, CHUNK - ii).astype(F32)
    rem_t = jnp.where(fwd, CHUNK - 1 - jj, jj).astype(F32)
    cos = cos_ref[...]
    sin = sin_ref[...]
    half = RET_QK_DIM // 2
    swap_lo = (lane % RET_QK_DIM) < half

    def rotary(tt):
        swapped = jnp.where(swap_lo, pltpu.roll(tt, LANES - half, axis=1), pltpu.roll(tt, half, axis=1))
        return tt * cos + swapped * sin

    for p in range(RET_HEADS // 2):
        sl = slice(p * LANES, (p + 1) * LANES)
        q_pair = rotary(rq_ref[0, :, sl].astype(F32))
        k_pair = rotary(rk_ref[0, :, sl].astype(F32) * (RET_QK_DIM ** -0.5))
        k_pair_bf = k_pair.astype(BF16)
        k_t = k_pair.T
        h_pair = hr_ref[p]
        h_bf = h_pair.astype(BF16)
        kts, vs, es = [], [], []
        for side, hh in enumerate((2 * p, 2 * p + 1)):
            lgb = jnp.broadcast_to(lg[hh:hh + 1, :], (CHUNK, LANES))
            keep = lane_lo if side == 0 else jnp.logical_not(lane_lo)
            keep_rows = row_lo if side == 0 else jnp.logical_not(row_lo)
            q_h = jnp.where(keep, q_pair, 0.0)
            scores = _dot_nt(q_h.astype(BF16), k_pair_bf)
            dec = jnp.where(incl, jnp.exp(lgb * dist), 0.0)
            m = (scores * dec).astype(BF16)
            qe = (q_h * jnp.exp(lgb * cnt)).astype(BF16)
            v_h = rv_ref[0, :, hh * RET_V_DIM:(hh + 1) * RET_V_DIM].astype(BF16)
            y = _dot(jnp.concatenate([m, qe], axis=1), jnp.concatenate([v_h, h_bf], axis=0))
            yret_ref[0, 0, :, hh * RET_V_DIM:(hh + 1) * RET_V_DIM] = y.astype(yret_ref.dtype)
            kts.append(jnp.where(keep_rows, k_t * jnp.exp(lgb * rem_t), 0.0).astype(BF16))
            vs.append(v_h)
            es.append(jnp.exp(lgb * float(CHUNK)))
        upd = _dot(jnp.concatenate(kts, axis=1), jnp.concatenate(vs, axis=0))
        hr_ref[p] = h_pair * jnp.where(row_lo, es[0], es[1]) + upd


def _even_scan(xbc, dtp, proj, cos_t, sin_t, dt_bias, a_log, ret_decay, *, n_ctx, rq_blk, rk_blk, rv_blk):
    b, l, _ = xbc.shape
    nc = l // CHUNK
    cmap = functools.partial(_chunk_of, n_ctx=n_ctx, n_all=nc)
    out_spec = pl.BlockSpec((1, 1, CHUNK, SSD_INNER), lambda bi, d, t: (bi, d, cmap(d, t), 0))
    return pl.pallas_call(
        _even_scan_kernel,
        grid=(b, 2, nc),
        in_specs=[
            pl.BlockSpec((1, CHUNK, SSD_INNER), lambda bi, d, t: (bi, cmap(d, t), 0)),
            pl.BlockSpec((1, CHUNK, 2 * SSD_GROUPS * SSD_STATE), lambda bi, d, t: (bi, cmap(d, t), 2)),
            pl.BlockSpec((1, CHUNK, LANES), lambda bi, d, t: (bi, cmap(d, t), d)),
            pl.BlockSpec((1, CHUNK, RET_QK), lambda bi, d, t: (bi, cmap(d, t), rq_blk)),
            pl.BlockSpec((1, CHUNK, RET_QK), lambda bi, d, t: (bi, cmap(d, t), rk_blk)),
            pl.BlockSpec((1, CHUNK, RET_V), lambda bi, d, t: (bi, cmap(d, t), rv_blk)),
            pl.BlockSpec((CHUNK, LANES), lambda bi, d, t: (cmap(d, t), 0)),
            pl.BlockSpec((CHUNK, LANES), lambda bi, d, t: (cmap(d, t), 0)),
            pl.BlockSpec((1, 1, LANES), lambda bi, d, t: (d, 0, 0)),
            pl.BlockSpec((1, 1, LANES), lambda bi, d, t: (d, 0, 0)),
            pl.BlockSpec((1, RET_HEADS, LANES), lambda bi, d, t: (d, 0, 0)),
        ],
        out_specs=[out_spec, out_spec],
        out_shape=[jax.ShapeDtypeStruct((b, 2, l, SSD_INNER), BF16),
                   jax.ShapeDtypeStruct((b, 2, l, RET_V), BF16)],
        scratch_shapes=[pltpu.VMEM((SSD_HEADS // 2, SSD_STATE, LANES), F32),
                        pltpu.VMEM((RET_HEADS // 2, LANES, RET_V_DIM), F32)],
        compiler_params=_cparams(("parallel", "parallel", "arbitrary")),
        name="even_scan",
    )(xbc, xbc, dtp, proj, proj, proj, cos_t, sin_t, dt_bias, a_log, ret_decay)


def _even_out_kernel(ys0_ref, ys1_ref, yr0_ref, yr1_ref, xs_ref, z_ref, rg_ref, dsk_ref, nw_ref, o_ref):
    def f32(ref, *idx):
        return ref[idx].astype(F32)

    y = (f32(ys0_ref, 0, 0) + f32(ys1_ref, 0, 0) + dsk_ref[...] * f32(xs_ref, 0)) * _silu(f32(z_ref, 0))
    gsz = SSD_INNER // SSD_GROUPS
    for g in range(SSD_GROUPS):
        sl = slice(g * gsz, (g + 1) * gsz)
        o_ref[0, :, sl] = _rms(y[:, sl], nw_ref[:, sl]).astype(BF16)
    yr = f32(yr0_ref, 0, 0) + f32(yr1_ref, 0, 0)
    gate = _silu(f32(rg_ref, 0))
    for hh in range(RET_HEADS):
        sl = slice(hh * RET_V_DIM, (hh + 1) * RET_V_DIM)
        v = yr[:, sl]
        mu = jnp.mean(v, axis=-1, keepdims=True)
        cen = v - mu
        var = jnp.mean(cen * cen, axis=-1, keepdims=True)
        o_ref[0, :, SSD_INNER + hh * RET_V_DIM:SSD_INNER + (hh + 1) * RET_V_DIM] = (
            cen * lax.rsqrt(var + EPS) * gate[:, sl]).astype(BF16)


EVEN_OUT_TILE = ROW_TILE


def _even_out(y_ssd, y_ret, xbc, proj, d_skip, ssd_norm, *, z_blk, rg_blk):
    b, _, l, _ = y_ssd.shape
    tr = EVEN_OUT_TILE
    w = SSD_INNER
    dsk = jnp.repeat(d_skip, SSD_HEAD_DIM).reshape(1, w)

    def dir_spec(dd):
        return pl.BlockSpec((1, 1, tr, w), lambda bi, j: (bi, dd, j, 0))

    def col_spec(blk):
        return pl.BlockSpec((1, tr, w), lambda bi, j: (bi, j, blk))

    vec = pl.BlockSpec((1, w), lambda bi, j: (0, 0))
    return pl.pallas_call(
        _even_out_kernel,
        grid=(b, l // tr),
        in_specs=[dir_spec(0), dir_spec(1), dir_spec(0), dir_spec(1), col_spec(0), col_spec(z_blk),
                  col_spec(rg_blk), vec, vec],
        out_specs=pl.BlockSpec((1, tr, 2 * w), lambda bi, j: (bi, j, 0)),
        out_shape=jax.ShapeDtypeStruct((b, l, 2 * w), BF16),
        compiler_params=_cparams(("parallel", "parallel")),
        name="even_out",
    )(y_ssd, y_ssd, y_ret, y_ret, xbc, proj, proj, dsk, ssd_norm.reshape(1, w))


GDN_LEVELS = 7
GDN_GROUP = 16


def _bdot(a, b):
    return jnp.einsum("gmk,gkn->gmn", a, b, preferred_element_type=F32)


def _bdot_nt(a, b):
    return jnp.einsum("gmk,gnk->gmn", a, b, preferred_element_type=F32)


def _gdn_scan_kernel(q_ref, k_ref, v_ref, raw_ref, bias_ref, alog_ref, lvl_ref, o_ref, s_ref):
    d = pl.program_id(1)
    t = pl.program_id(2)

    @pl.when(t == 0)
    def _():
        s_ref[...] = jnp.zeros_like(s_ref)

    ii, jj, fwd, incl, strict = _scan_masks(d)
    strict_f = jnp.where(strict, 1.0, 0.0)
    eye = jnp.where(ii == jj, 1.0, 0.0)
    tri = jnp.where(incl, 1.0, 0.0).astype(BF16)

    raw = raw_ref[0]
    beta = jax.nn.sigmoid(raw)
    gl = -jnp.exp(alog_ref[0]) * _softplus(raw + bias_ref[0])
    cs = _dot_sel_left(tri, gl)
    tot = jnp.sum(gl, axis=0, keepdims=True)
    cs_t = cs.T
    wend_t = jnp.exp(tot - cs).T

    gsz = GDN_GROUP
    rep = GDN_V_HEADS // GDN_K_HEADS
    kper = gsz // rep
    hd = GDN_HEAD_DIM
    strict_f3 = strict_f[None]
    incl3 = incl[None]

    def col_bcast(a, c):
        return jnp.broadcast_to(a[:, c:c + 1], (CHUNK, LANES))

    def per_key_head(a4, x):
        res = a4[:, None] * x.reshape((kper, rep) + x.shape[1:])
        return res.reshape((gsz,) + res.shape[2:])

    for gi in range(GDN_V_HEADS // gsz):
        heads = range(gi * gsz, (gi + 1) * gsz)
        kheads = range(gi * kper, (gi + 1) * kper)
        q4b = jnp.stack([q_ref[0, :, kh * hd:(kh + 1) * hd] for kh in kheads])
        k4b = jnp.stack([k_ref[0, :, kh * hd:(kh + 1) * hd] for kh in kheads])
        k4 = k4b.astype(F32)
        kk4 = _bdot_nt(k4b, k4b)
        qk4 = _bdot_nt(q4b, k4b)
        kt4 = jnp.stack([k4[i].T for i in range(kper)])
        v = jnp.stack([v_ref[0, :, h * hd:(h + 1) * hd] for h in heads])
        cs_b = jnp.stack([col_bcast(cs, GDN_V_HEADS + h) for h in heads])
        beta_b = jnp.stack([col_bcast(beta, h) for h in heads])
        cs_r = jnp.stack([cs_t[GDN_V_HEADS + h:GDN_V_HEADS + h + 1, :] for h in heads])
        wend_r = jnp.stack([wend_t[GDN_V_HEADS + h:GDN_V_HEADS + h + 1, :] for h in heads])
        gam = jnp.exp(jnp.where(incl3, cs_b - cs_r, NEG_BIG))
        ecs_b = jnp.exp(cs_b)
        etot = jnp.exp(jnp.where(fwd, cs_b[:, CHUNK - 1:CHUNK, :], cs_b[:, 0:1, :]))
        low = per_key_head(kk4, gam * (strict_f3 * beta_b))
        tinv = eye[None] - low * lvl_ref[0][None]
        for lv in range(1, GDN_LEVELS):
            a_off = (low * lvl_ref[lv][None]).astype(BF16)
            x = _bdot(a_off, tinv.astype(BF16))
            tinv = tinv - _bdot(tinv.astype(BF16), x.astype(BF16))
        state = s_ref[gi * gsz:(gi + 1) * gsz]
        ke = per_key_head(k4, ecs_b).astype(BF16)
        qe = per_key_head(q4b.astype(F32), ecs_b).astype(BF16)
        kq_s = _bdot(jnp.concatenate([ke, qe], axis=1), state.astype(BF16))
        resid = (beta_b * (v.astype(F32) - kq_s[:, :CHUNK])).astype(BF16)
        v_new_bf = _bdot(tinv.astype(BF16), resid).astype(BF16)
        out = kq_s[:, CHUNK:] + _bdot(per_key_head(qk4, gam).astype(BF16), v_new_bf)
        for g, h in enumerate(heads):
            o_ref[0, 0, :, h * hd:(h + 1) * hd] = out[g].astype(o_ref.dtype)
        k_end_t = per_key_head(kt4, wend_r).astype(BF16)
        s_ref[gi * gsz:(gi + 1) * gsz] = state * etot + _bdot(k_end_t, v_new_bf)


def _gdn_level_masks():
    i = jnp.arange(CHUNK)[:, None]
    j = jnp.arange(CHUNK)[None, :]
    masks = []
    for lv in range(GDN_LEVELS):
        blk = 1 << lv
        masks.append(((i // (2 * blk) == j // (2 * blk)) & (i // blk != j // blk)).astype(F32))
    return jnp.stack(masks)


def _gdn_scan(qkv, raw, bias, a_log, *, n_ctx):
    b, l, _ = qkv.shape
    nc = l // CHUNK
    cmap = functools.partial(_chunk_of, n_ctx=n_ctx, n_all=nc)
    return pl.pallas_call(
        _gdn_scan_kernel,
        grid=(b, 2, nc),
        in_specs=[
            pl.BlockSpec((1, CHUNK, GDN_K), lambda bi, d, t: (bi, cmap(d, t), 0)),
            pl.BlockSpec((1, CHUNK, GDN_K), lambda bi, d, t: (bi, cmap(d, t), 1)),
            pl.BlockSpec((1, CHUNK, GDN_V), lambda bi, d, t: (bi, cmap(d, t), 1)),
            pl.BlockSpec((1, CHUNK, LANES), lambda bi, d, t: (bi, cmap(d, t), d)),
            pl.BlockSpec((1, 1, LANES), lambda bi, d, t: (d, 0, 0)),
            pl.BlockSpec((1, 1, LANES), lambda bi, d, t: (d, 0, 0)),
            pl.BlockSpec((GDN_LEVELS, CHUNK, CHUNK), lambda bi, d, t: (0, 0, 0)),
        ],
        out_specs=pl.BlockSpec((1, 1, CHUNK, GDN_V), lambda bi, d, t: (bi, d, cmap(d, t), 0)),
        out_shape=jax.ShapeDtypeStruct((b, 2, l, GDN_V), BF16),
        scratch_shapes=[pltpu.VMEM((GDN_V_HEADS, GDN_HEAD_DIM, GDN_HEAD_DIM), F32)],
        compiler_params=_cparams(("parallel", "parallel", "arbitrary")),
        name="gdn_scan",
    )(qkv, qkv, qkv, raw, bias, a_log, _gdn_level_masks())


def _odd_out_kernel(o0_ref, o1_ref, z_ref, nw_ref, out_ref):
    o = o0_ref[0, 0].astype(F32) + o1_ref[0, 0].astype(F32)
    gate = _silu(z_ref[0].astype(F32))
    for hh in range(GDN_V_HEADS):
        sl = slice(hh * GDN_HEAD_DIM, (hh + 1) * GDN_HEAD_DIM)
        out_ref[0, :, sl] = (_rms(o[:, sl], nw_ref[...]) * gate[:, sl]).astype(BF16)


def _odd_out(o, proj, norm_w, *, z_blk):
    b, _, l, w = o.shape
    tr = ROW_TILE

    def dir_spec(dd):
        return pl.BlockSpec((1, 1, tr, w), lambda bi, j: (bi, dd, j, 0))

    return pl.pallas_call(
        _odd_out_kernel,
        grid=(b, l // tr),
        in_specs=[dir_spec(0), dir_spec(1),
                  pl.BlockSpec((1, tr, w), lambda bi, j: (bi, j, z_blk)),
                  pl.BlockSpec((1, GDN_HEAD_DIM), lambda bi, j: (0, 0))],
        out_specs=pl.BlockSpec((1, tr, w), lambda bi, j: (bi, j, 0)),
        out_shape=jax.ShapeDtypeStruct((b, l, w), BF16),
        compiler_params=_cparams(("parallel", "parallel")),
        name="odd_out",
    )(o, o, proj, norm_w.reshape(1, GDN_HEAD_DIM))


def _dir_lanes(p, lane_off):
    out = jnp.zeros((2, 1, LANES), F32)
    return out.at[:, 0, lane_off:lane_off + p.shape[1]].set(p.astype(F32))


def _rotary_tables(n_ctx_rows, n_lat_rows):
    nf = RET_QK_DIM // 4
    pos = jnp.arange(n_lat_rows)
    rowp = (pos // GRID_W).astype(F32)
    colp = (pos % GRID_W).astype(F32)
    freqs = ROPE_BASE ** (-jnp.arange(nf, dtype=F32) / nf)
    ang = jnp.concatenate([rowp[:, None] * freqs, colp[:, None] * freqs], axis=-1)
    cos = jnp.cos(ang)
    sin = jnp.sin(ang)
    cos_h = jnp.concatenate([cos, cos], axis=-1)
    sin_h = jnp.concatenate([-sin, sin], axis=-1)
    reps = LANES // RET_QK_DIM
    cos_t = jnp.concatenate([jnp.ones((n_ctx_rows, LANES), F32), jnp.tile(cos_h, (1, reps))], axis=0)
    sin_t = jnp.concatenate([jnp.zeros((n_ctx_rows, LANES), F32), jnp.tile(sin_h, (1, reps))], axis=0)
    return cos_t, sin_t


EV_Z, EV_RG, EV_RV, EV_XBC, EV_RQ, EV_RK = 0, 2048, 4096, 6144, 9216, 10240
EV_MAIN = 11264
SSD_CONV_DIM = SSD_INNER + 2 * SSD_GROUPS * SSD_STATE


def _even_weights(w_in):
    o = 0
    parts = {}
    for name, sz in (("z", SSD_INNER), ("xbc", SSD_CONV_DIM), ("dt", 2 * SSD_HEADS), ("rq", RET_QK),
                     ("rk", RET_QK), ("rv", RET_V), ("rg", RET_V)):
        parts[name] = w_in[:, o:o + sz]
        o += sz
    main = jnp.concatenate([parts[nm] for nm in ("z", "rg", "rv", "xbc", "rq", "rk")], axis=1).astype(BF16)
    pad = jnp.zeros((w_in.shape[0], LANES - SSD_HEADS), w_in.dtype)
    dt = parts["dt"]
    w_dt = jnp.concatenate([dt[:, :SSD_HEADS], pad, dt[:, SSD_HEADS:], pad], axis=1)
    return main[None], w_dt[None]


def _odd_small_weights(w_in):
    base = 2 * GDN_K + 2 * GDN_V
    hv = GDN_V_HEADS
    beta = w_in[:, base:base + 2 * hv]
    dec = w_in[:, base + 2 * hv:base + 4 * hv]
    pad = jnp.zeros((w_in.shape[0], LANES - 2 * hv), w_in.dtype)
    return jnp.concatenate([beta[:, :hv], dec[:, :hv], pad, beta[:, hv:], dec[:, hv:], pad], axis=1)[None]


def _ffn(s, o, mod, layer, post_mix, pre_ffn, w1, w3, w2):
    b, l, d = s.shape
    s1, h = _res_norm(s, mod, o=o, gate=(layer, 2), post_w=post_mix, pre_w=pre_ffn,
                      shift=(layer, 3), scale=(layer, 4))
    d_ff = w1.shape[2]
    act = _matmul(h.reshape(b * l, d), [w1, w3], layer=layer, n=d_ff, tn=256, out_dtype=BF16)
    f = _matmul_ktiled(act, w2, layer=layer)
    return s1, f.reshape(b, l, d)


def kernel(x, c, ctx, c_ctx, ada_w, ada_b, norm_mix_pre, norm_mix_post, norm_ffn_pre, norm_ffn_post,
           ev_w_in, ev_conv_w, ev_conv_b, ev_dt_bias, ev_a_log, ev_d_skip, ev_ssd_norm, ev_ret_decay,
           ev_w_out, od_w_in, od_conv_w, od_dt_bias, od_a_log, od_norm, od_w_out,
           ffn_w1, ffn_w3, ffn_w2):
    b, seq, d = x.shape
    n_ctx_rows = ctx.shape[1]
    assert n_ctx_rows == ROW_TILE and seq % ROW_TILE == 0
    l = n_ctx_rows + seq
    m = b * l
    n_ctx = n_ctx_rows // CHUNK

    s = jnp.concatenate([ctx, x], axis=1)
    mod = _ada(jnp.concatenate([c, c_ctx[None]], axis=0), ada_w, ada_b)

    _, h = _res_norm(s, mod, pre_w=norm_mix_pre[0], shift=(0, 0), scale=(0, 1))
    w_main, w_dt = _even_weights(ev_w_in[0])
    h2 = h.reshape(m, d)
    proj = _matmul(h2, [w_main], n=EV_MAIN, tn=512, out_dtype=BF16).reshape(b, l, EV_MAIN)
    dtp = _matmul(h2, [w_dt], n=2 * LANES, tn=2 * LANES, out_dtype=F32).reshape(b, l, 2 * LANES)
    xbc = _conv_silu(proj, ev_conv_w[0], ev_conv_b[0], col_off=EV_XBC, width=SSD_CONV_DIM)
    cos_t, sin_t = _rotary_tables(n_ctx_rows, seq)
    rdec = jnp.broadcast_to(ev_ret_decay[0][:, :, None], (2, RET_HEADS, LANES)).astype(F32)
    y_ssd, y_ret = _even_scan(xbc, dtp, proj, cos_t, sin_t, _dir_lanes(ev_dt_bias[0], 0),
                              _dir_lanes(ev_a_log[0], 0), rdec, n_ctx=n_ctx,
                              rq_blk=EV_RQ // RET_QK, rk_blk=EV_RK // RET_QK, rv_blk=EV_RV // RET_V)
    act = _even_out(y_ssd, y_ret, xbc, proj, ev_d_skip[0], ev_ssd_norm[0],
                    z_blk=EV_Z // SSD_INNER, rg_blk=EV_RG // SSD_INNER)
    o = _matmul(act.reshape(m, act.shape[-1]), [ev_w_out], layer=0, n=d, tn=512, out_dtype=BF16).reshape(b, l, d)
    s, f = _ffn(s, o, mod, 0, norm_mix_post[0], norm_ffn_pre[0], ffn_w1, ffn_w3, ffn_w2)

    s, h = _res_norm(s, mod, o=f, gate=(0, 5), post_w=norm_ffn_post[0], pre_w=norm_mix_pre[1],
                     shift=(1, 0), scale=(1, 1))
    h2 = h.reshape(m, d)
    n_main = 2 * GDN_K + 2 * GDN_V
    proj = _matmul(h2, [od_w_in], layer=0, n=n_main, tn=512, out_dtype=BF16).reshape(b, l, n_main)
    raw = _matmul(h2, [_odd_small_weights(od_w_in[0])], n=2 * LANES, tn=2 * LANES,
                  out_dtype=F32).reshape(b, l, 2 * LANES)
    qkv = _conv_silu(proj, od_conv_w[0], None, col_off=0, width=2 * GDN_K + GDN_V,
                     l2_blocks=2 * GDN_K // CONV_CW, l2_scaled_blocks=GDN_K // CONV_CW,
                     l2_scale=GDN_HEAD_DIM ** -0.5)
    o2 = _gdn_scan(qkv, raw, _dir_lanes(od_dt_bias[0], GDN_V_HEADS), _dir_lanes(od_a_log[0], GDN_V_HEADS),
                   n_ctx=n_ctx)
    act = _odd_out(o2, proj, od_norm[0], z_blk=(2 * GDN_K + GDN_V) // GDN_V)
    o = _matmul(act.reshape(m, act.shape[-1]), [od_w_out], layer=0, n=d, tn=512, out_dtype=BF16).reshape(b, l, d)
    s, f = _ffn(s, o, mod, 1, norm_mix_post[1], norm_ffn_pre[1], ffn_w1, ffn_w3, ffn_w2)

    out, _ = _res_norm(s, mod, o=f, gate=(1, 5), post_w=norm_ffn_post[1], row_off=1, out_rows=seq)
    return out
```

```python
import functools
import math

import jax
import jax.numpy as jnp
from jax import lax
from jax.experimental import pallas as pl
from jax.experimental.pallas import tpu as pltpu

F32 = jnp.float32
BF16 = jnp.bfloat16

EPS = 1e-6
CHUNK = 128
LANES = 128
CONV_W = 5
CONV_HALO = 16
GRID_W = 64
ROPE_BASE = 10000.0
ROW_TILE = 256
NEG_BIG = -1e30

SSD_HEADS = 32
SSD_HEAD_DIM = 64
SSD_GROUPS = 4
SSD_STATE = 128
SSD_INNER = SSD_HEADS * SSD_HEAD_DIM
RET_HEADS = 16
RET_QK_DIM = 64
RET_V_DIM = 128
RET_QK = RET_HEADS * RET_QK_DIM
RET_V = RET_HEADS * RET_V_DIM
GDN_K_HEADS = 16
GDN_V_HEADS = 32
GDN_HEAD_DIM = 128
GDN_K = GDN_K_HEADS * GDN_HEAD_DIM
GDN_V = GDN_V_HEADS * GDN_HEAD_DIM

VMEM_LIMIT = 56 * 1024 * 1024


def _cparams(sem):
    return pltpu.CompilerParams(dimension_semantics=sem, vmem_limit_bytes=VMEM_LIMIT)


def _dot(a, b):
    return jnp.dot(a, b, preferred_element_type=F32)


def _dot_nt(a, b):
    return lax.dot_general(a, b, (((1,), (1,)), ((), ())), preferred_element_type=F32)


def _split3(x):
    hi = x.astype(BF16)
    r1 = x - hi.astype(F32)
    mid = r1.astype(BF16)
    lo = (r1 - mid.astype(F32)).astype(BF16)
    return hi, mid, lo


def _dot_sel_left(sel, x):
    hi, mid, lo = _split3(x)
    return _dot(sel, hi) + _dot(sel, mid) + _dot(sel, lo)


def _dot_sel_right(x, sel):
    hi, mid, lo = _split3(x)
    return _dot(hi, sel) + _dot(mid, sel) + _dot(lo, sel)


def _silu(x):
    return x * jax.nn.sigmoid(x)


def _softplus(x):
    return jnp.maximum(x, 0.0) + jnp.log(1.0 + jnp.exp(-jnp.abs(x)))


def _rms(x, w):
    ms = jnp.mean(x * x, axis=-1, keepdims=True)
    return x * lax.rsqrt(ms + EPS) * w


ADA_TN = 512


def _ada_kernel(cb_ref, w_ref, b_ref, o_ref, act_ref, *, n_rows):
    first = jnp.logical_and(pl.program_id(0) == 0, pl.program_id(1) == 0)

    @pl.when(first)
    def _():
        act_ref[...] = _silu(cb_ref[...])

    d = w_ref.shape[1]
    tn = w_ref.shape[2]
    w3 = w_ref[0].reshape(d // 8, 8, tn)
    o_ref[...] = jnp.zeros_like(o_ref)
    for r in range(n_rows):
        a3 = act_ref[r].reshape(d // 8, 8, LANES)
        for lt in range(tn // LANES):
            sl = slice(lt * LANES, (lt + 1) * LANES)
            part = jnp.sum(w3[:, :, sl] * a3, axis=0)
            o_ref[0, r:r + 1, sl] = jnp.sum(part, axis=0, keepdims=True) + b_ref[0, :, sl]


def _ada(cvec, ada_w, ada_b):
    n_rows, d = cvec.shape
    depth, _, n = ada_w.shape
    cb = jnp.broadcast_to(cvec[:, :, None], (n_rows, d, LANES))
    return pl.pallas_call(
        functools.partial(_ada_kernel, n_rows=n_rows),
        grid=(depth, n // ADA_TN),
        in_specs=[
            pl.BlockSpec((n_rows, d, LANES), lambda i, j: (0, 0, 0)),
            pl.BlockSpec((1, d, ADA_TN), lambda i, j: (i, 0, j)),
            pl.BlockSpec((1, 1, ADA_TN), lambda i, j: (i, 0, j)),
        ],
        out_specs=pl.BlockSpec((1, 8, ADA_TN), lambda i, j: (i, 0, j)),
        out_shape=jax.ShapeDtypeStruct((depth, 8, n), F32),
        scratch_shapes=[pltpu.VMEM((n_rows, d, LANES), F32)],
        compiler_params=_cparams(("arbitrary", "arbitrary")),
        name="ada_mod",
    )(cb, ada_w, ada_b.reshape(depth, 1, n))


def _mod_row(m_ref, ctx_row):
    r = pl.program_id(0)
    if ctx_row is not None:
        r = jnp.where(pl.program_id(1) == 0, ctx_row, r)
    return m_ref[0, pl.ds(r, 1), :]


def _res_norm_kernel(*refs, split_in, has_res, has_h, ctx_row):
    it = iter(refs)
    if split_in:
        ctx_ref, x_ref = next(it), next(it)
        s = jnp.where(pl.program_id(1) == 0, ctx_ref[0], x_ref[0])
    else:
        s = next(it)[0]
    if has_res:
        o_ref, gate_ref, post_ref = next(it), next(it), next(it)
    if has_h:
        pre_ref, shift_ref, scale_ref = next(it), next(it), next(it)
    if has_res:
        s = s + _mod_row(gate_ref, ctx_row) * _rms(o_ref[0].astype(F32), post_ref[...])
    if has_res or split_in:
        next(it)[0] = s
    if has_h:
        h = _rms(s, pre_ref[...]) * (1.0 + _mod_row(scale_ref, ctx_row)) + _mod_row(shift_ref, ctx_row)
        next(it)[0] = h.astype(BF16)


def _res_norm(s, mod, *, o=None, gate=None, post_w=None, pre_w=None, shift=None, scale=None,
              s_off=0, o_off=0, n_tiles=None, has_ctx=True):
    split_in = isinstance(s, tuple)
    b, _, d = (s[1] if split_in else s).shape
    has_res = o is not None
    has_h = pre_w is not None
    if n_tiles is None:
        n_tiles = (s[0].shape[1] + s[1].shape[1]) // ROW_TILE if split_in else s.shape[1] // ROW_TILE - s_off
    ctx_row = b if has_ctx else None

    def outrows(bi, j):
        return (bi, j, 0)

    def mod_spec(sel):
        layer, comp = sel
        return pl.BlockSpec((1, 8, d), lambda bi, j: (layer, 0, comp))

    vec_spec = pl.BlockSpec((1, d), lambda bi, j: (0, 0))
    tile = (1, ROW_TILE, d)
    if split_in:
        assert s[0].shape[1] == ROW_TILE and has_ctx and not has_res
        args = list(s)
        in_specs = [pl.BlockSpec(tile, lambda bi, j: (bi, 0, 0)),
                    pl.BlockSpec(tile, lambda bi, j: (bi, jnp.maximum(j - 1, 0), 0))]
    else:
        args, in_specs = [s], [pl.BlockSpec(tile, lambda bi, j: (bi, j + s_off, 0))]
    out_specs, out_shape = [], []
    if has_res:
        args += [o, mod, post_w.reshape(1, d)]
        in_specs += [pl.BlockSpec(tile, lambda bi, j: (bi, j + o_off, 0)), mod_spec(gate), vec_spec]
    if has_res or split_in:
        out_specs.append(pl.BlockSpec(tile, outrows))
        out_shape.append(jax.ShapeDtypeStruct((b, n_tiles * ROW_TILE, d), F32))
    if has_h:
        args += [pre_w.reshape(1, d), mod, mod]
        in_specs += [vec_spec, mod_spec(shift), mod_spec(scale)]
        out_specs.append(pl.BlockSpec(tile, outrows))
        out_shape.append(jax.ShapeDtypeStruct((b, n_tiles * ROW_TILE, d), BF16))
    outs = pl.pallas_call(
        functools.partial(_res_norm_kernel, split_in=split_in, has_res=has_res, has_h=has_h, ctx_row=ctx_row),
        grid=(b, n_tiles),
        in_specs=in_specs,
        out_specs=out_specs,
        out_shape=out_shape,
        compiler_params=_cparams(("parallel", "parallel")),
        name="res_norm",
    )(*args)
    outs = list(outs)
    s_new = outs.pop(0) if (has_res or split_in) else None
    h = outs.pop(0) if has_h else None
    return s_new, h


MM_TM = 2048


def _row_tile(m, cap=MM_TM):
    best = ROW_TILE
    for tm in range(ROW_TILE, cap + 1, ROW_TILE):
        if m % tm == 0:
            best = tm
    return best


def _mm_kernel(a_ref, *refs, swiglu):
    a = a_ref[...]
    if swiglu:
        w1_ref, w3_ref, o_ref = refs
        g = _dot(a, w1_ref[0].astype(BF16))
        u = _dot(a, w3_ref[0].astype(BF16))
        o_ref[...] = (_silu(g) * u).astype(o_ref.dtype)
    else:
        w_ref, o_ref = refs
        o_ref[...] = _dot(a, w_ref[0].astype(BF16)).astype(o_ref.dtype)


def _matmul(a, ws, *, layer=0, col_off=0, n, tn, out_dtype):
    m, k = a.shape
    swiglu = len(ws) == 2
    tm = _row_tile(m)
    assert col_off % tn == 0
    off = col_off // tn
    w_spec = pl.BlockSpec((1, k, tn), lambda i, j: (layer, 0, j + off))
    return pl.pallas_call(
        functools.partial(_mm_kernel, swiglu=swiglu),
        grid=(m // tm, pl.cdiv(n, tn)),
        in_specs=[pl.BlockSpec((tm, k), lambda i, j: (i, 0), pipeline_mode=pl.Buffered(1))]
        + [w_spec] * len(ws),
        out_specs=pl.BlockSpec((tm, tn), lambda i, j: (i, j)),
        out_shape=jax.ShapeDtypeStruct((m, n), out_dtype),
        compiler_params=_cparams(("parallel", "arbitrary")),
        name="matmul_swiglu" if swiglu else "matmul",
    )(a, *ws)


MMK_TN = 1024
MMK_TK = 1024


def _mm_ktiled_kernel(a_ref, w_ref, o_ref, acc_ref, *, k_total, tk):
    kk = pl.program_id(2)
    nk = pl.num_programs(2)

    @pl.when(kk == 0)
    def _():
        acc_ref[...] = jnp.zeros_like(acc_ref)

    rem = k_total - (pl.cdiv(k_total, tk) - 1) * tk

    def step(a, w):
        acc_ref[...] += _dot(a, w.astype(BF16))

    if rem == tk:
        step(a_ref[...], w_ref[0])
    else:
        @pl.when(kk < nk - 1)
        def _():
            step(a_ref[...], w_ref[0])

        @pl.when(kk == nk - 1)
        def _():
            a = a_ref[...]
            w = w_ref[0]
            ka = lax.broadcasted_iota(jnp.int32, a.shape, 1)
            kw = lax.broadcasted_iota(jnp.int32, w.shape, 0)
            step(jnp.where(ka < rem, a, jnp.zeros_like(a)), jnp.where(kw < rem, w, jnp.zeros_like(w)))

    @pl.when(kk == nk - 1)
    def _():
        o_ref[...] = acc_ref[...].astype(o_ref.dtype)


def _matmul_ktiled(a, w, *, layer, tk=MMK_TK):
    m, k = a.shape
    n = w.shape[2]
    tm = _row_tile(m)
    tn = min(MMK_TN, n)
    assert n % tn == 0
    return pl.pallas_call(
        functools.partial(_mm_ktiled_kernel, k_total=k, tk=tk),
        grid=(m // tm, n // tn, pl.cdiv(k, tk)),
        in_specs=[pl.BlockSpec((tm, tk), lambda i, j, kk: (i, kk)),
                  pl.BlockSpec((1, tk, tn), lambda i, j, kk: (layer, kk, j))],
        out_specs=pl.BlockSpec((tm, tn), lambda i, j, kk: (i, j)),
        out_shape=jax.ShapeDtypeStruct((m, n), BF16),
        scratch_shapes=[pltpu.VMEM((tm, tn), F32)],
        compiler_params=_cparams(("parallel", "parallel", "arbitrary")),
        name="matmul_ktiled",
    )(a, w)


CONV_CW = 1024


def _conv_kernel(cur_ref, prev_ref, next_ref, w_ref, *refs, has_bias, l2_blocks, l2_scaled_blocks,
                 l2_scale):
    if has_bias:
        b_ref, o_ref, buf_ref = refs
    else:
        o_ref, buf_ref = refs
    j = pl.program_id(1)
    nj = pl.num_programs(1)
    tr = cur_ref.shape[1]
    use_prev = jnp.logical_and(j != 0, j != 1)
    use_next = jnp.logical_and(j != 0, j != nj - 1)
    zeros = jnp.zeros(prev_ref.shape[1:], F32)
    buf_ref[0:CONV_HALO, :] = jnp.where(use_prev, prev_ref[0].astype(F32), zeros)
    buf_ref[CONV_HALO:CONV_HALO + tr, :] = cur_ref[0].astype(F32)
    buf_ref[CONV_HALO + tr:2 * CONV_HALO + tr, :] = jnp.where(use_next, next_ref[0].astype(F32), zeros)
    base = CONV_HALO - CONV_W // 2
    acc = w_ref[0:1, :] * buf_ref[base:base + tr, :]
    for t in range(1, CONV_W):
        acc = acc + w_ref[t:t + 1, :] * buf_ref[base + t:base + t + tr, :]
    if has_bias:
        acc = acc + b_ref[...]
    y = _silu(acc)
    if l2_blocks == 0:
        o_ref[0] = y.astype(o_ref.dtype)
    else:
        cblk = pl.program_id(2)

        @pl.when(cblk >= l2_blocks)
        def _():
            o_ref[0] = y.astype(o_ref.dtype)

        @pl.when(cblk < l2_blocks)
        def _():
            scale = jnp.where(cblk < l2_scaled_blocks, l2_scale, 1.0).astype(F32)
            for hd in range(y.shape[1] // LANES):
                sl = slice(hd * LANES, (hd + 1) * LANES)
                yh = y[:, sl]
                ss = jnp.sum(yh * yh, axis=-1, keepdims=True)
                o_ref[0, :, sl] = (yh * (lax.rsqrt(ss + EPS) * scale)).astype(o_ref.dtype)


def _conv_silu(proj, conv_w, conv_b, *, col_off, width, l2_blocks=0, l2_scaled_blocks=0, l2_scale=1.0):
    b, l, _ = proj.shape
    assert col_off % CONV_CW == 0 and width % CONV_CW == 0 and l % ROW_TILE == 0
    coff = col_off // CONV_CW
    hpt = ROW_TILE // CONV_HALO
    n_halo = l // CONV_HALO
    has_bias = conv_b is not None
    args = [proj, proj, proj, conv_w]
    in_specs = [
        pl.BlockSpec((1, ROW_TILE, CONV_CW), lambda bi, j, c: (bi, j, c + coff)),
        pl.BlockSpec((1, CONV_HALO, CONV_CW), lambda bi, j, c: (bi, jnp.maximum(j * hpt - 1, 0), c + coff)),
        pl.BlockSpec((1, CONV_HALO, CONV_CW),
                     lambda bi, j, c: (bi, jnp.minimum((j + 1) * hpt, n_halo - 1), c + coff)),
        pl.BlockSpec((CONV_W, CONV_CW), lambda bi, j, c: (0, c)),
    ]
    if has_bias:
        args.append(conv_b.reshape(1, width))
        in_specs.append(pl.BlockSpec((1, CONV_CW), lambda bi, j, c: (0, c)))
    return pl.pallas_call(
        functools.partial(_conv_kernel, has_bias=has_bias, l2_blocks=l2_blocks,
                          l2_scaled_blocks=l2_scaled_blocks, l2_scale=l2_scale),
        grid=(b, l // ROW_TILE, width // CONV_CW),
        in_specs=in_specs,
        out_specs=pl.BlockSpec((1, ROW_TILE, CONV_CW), lambda bi, j, c: (bi, j, c)),
        out_shape=jax.ShapeDtypeStruct((b, l, width), BF16),
        scratch_shapes=[pltpu.VMEM((ROW_TILE + 2 * CONV_HALO, CONV_CW), F32)],
        compiler_params=_cparams(("parallel", "parallel", "parallel")),
        name="conv_silu",
    )(*args)


def _chunk_of(d, t, n_ctx, n_all):
    rev = jnp.where(t < n_ctx, n_ctx - 1 - t, n_all - 1 - (t - n_ctx))
    return jnp.where(d == 0, t, rev)


def _scan_masks(d, reps=1):
    ii = lax.broadcasted_iota(jnp.int32, (CHUNK, reps * CHUNK), 0)
    jj = lax.broadcasted_iota(jnp.int32, (CHUNK, reps * CHUNK), 1) % CHUNK
    fwd = d == 0
    later = jnp.where(fwd, ii, jj)
    earlier = jnp.where(fwd, jj, ii)
    incl = earlier <= later
    strict = earlier < later
    return ii, jj, fwd, incl, strict


def _even_scan_kernel(xs_ref, bc_ref, dt_ref, rq_ref, rk_ref, rv_ref, cos_ref, sin_ref,
                      dtb_ref, alog_ref, rdec_ref, yssd_ref, yret_ref, hs_ref, hr_ref):
    d = pl.program_id(1)
    t = pl.program_id(2)

    @pl.when(t == 0)
    def _():
        hs_ref[...] = jnp.zeros_like(hs_ref)
        hr_ref[...] = jnp.zeros_like(hr_ref)

    ii, jj, fwd, incl, _ = _scan_masks(d)
    lane = lax.broadcasted_iota(jnp.int32, (CHUNK, LANES), 1)
    row = lax.broadcasted_iota(jnp.int32, (CHUNK, LANES), 0)
    lane_lo = lane < (LANES // 2)
    row_lo = row < (CHUNK // 2)

    dtf = _softplus(dt_ref[0] + dtb_ref[0])
    la = -jnp.exp(alog_ref[0]) * dtf
    tri = jnp.where(incl, 1.0, 0.0).astype(BF16)
    cs = _dot_sel_left(tri, la)
    tot = jnp.sum(la, axis=0, keepdims=True)
    cs_t = cs.T
    dt_t = dtf.T
    wend_t = (dtf * jnp.exp(tot - cs)).T

    bc = bc_ref[0].astype(F32)
    n_state = SSD_GROUPS * SSD_STATE
    for g in range(SSD_GROUPS):
        bm = bc[:, g * SSD_STATE:(g + 1) * SSD_STATE]
        cm = bc[:, n_state + g * SSD_STATE:n_state + (g + 1) * SSD_STATE]
        bm_bf = bm.astype(BF16)
        cm_bf = cm.astype(BF16)
        scores = _dot_nt(cm_bf, bm_bf)
        bm_t = bm.T
        heads_per_group = SSD_HEADS // SSD_GROUPS
        for pp in range(heads_per_group // 2):
            pidx = g * (heads_per_group // 2) + pp
            xs_pair = xs_ref[0, :, pidx * LANES:(pidx + 1) * LANES]
            x_lo = jnp.where(lane_lo, xs_pair, 0.0).astype(BF16)
            x_hi = jnp.where(lane_lo, 0.0, xs_pair).astype(BF16)
            h_pair = hs_ref[pidx]
            h_lo = jnp.where(lane_lo, h_pair, 0.0).astype(BF16)
            h_hi = jnp.where(lane_lo, 0.0, h_pair).astype(BF16)
            ms, qs, kws, ets = [], [], [], []
            for r in (2 * pidx, 2 * pidx + 1):
                cs_b = jnp.broadcast_to(cs[:, r:r + 1], (CHUNK, LANES))
                dec = jnp.exp(jnp.where(incl, cs_b - cs_t[r:r + 1, :], NEG_BIG))
                ms.append((scores * dec * dt_t[r:r + 1, :]).astype(BF16))
                qs.append((cm * jnp.exp(cs_b)).astype(BF16))
                kws.append((bm_t * wend_t[r:r + 1, :]).astype(BF16))
                ets.append(jnp.exp(jnp.where(fwd, cs_b[CHUNK - 1:CHUNK, :], cs_b[0:1, :])))
            lhs = jnp.concatenate(ms + qs, axis=1)
            rhs = jnp.concatenate([x_lo, x_hi, h_lo, h_hi], axis=0)
            yssd_ref[0, 0, :, pidx * LANES:(pidx + 1) * LANES] = _dot(lhs, rhs).astype(yssd_ref.dtype)
            et = jnp.where(lane_lo, jnp.broadcast_to(ets[0], (CHUNK, LANES)),
                           jnp.broadcast_to(ets[1], (CHUNK, LANES)))
            upd = _dot(jnp.concatenate(kws, axis=1), jnp.concatenate([x_lo, x_hi], axis=0))
            hs_ref[pidx] = h_pair * et + upd

    lg = jax.nn.log_sigmoid(rdec_ref[0])
    dist = jnp.where(fwd, ii - jj, jj - ii).astype(F32)
    cnt = jnp.where(fwd, ii + 1, CHUNK - ii).astype(F32)
    rem_t = jnp.where(fwd, CHUNK - 1 - jj, jj).astype(F32)
    cos = cos_ref[...]
    sin = sin_ref[...]
    half = RET_QK_DIM // 2
    swap_lo = (lane % RET_QK_DIM) < half

    def rotary(tt):
        swapped = jnp.where(swap_lo, pltpu.roll(tt, LANES - half, axis=1), pltpu.roll(tt, half, axis=1))
        return tt * cos + swapped * sin

    for p in range(RET_HEADS // 2):
        sl = slice(p * LANES, (p + 1) * LANES)
        q_pair = rotary(rq_ref[0, :, sl].astype(F32))
        k_pair = rotary(rk_ref[0, :, sl].astype(F32) * (RET_QK_DIM ** -0.5))
        k_pair_bf = k_pair.astype(BF16)
        k_t = k_pair.T
        h_pair = hr_ref[p]
        h_bf = h_pair.astype(BF16)
        kts, vs, es = [], [], []
        for side, hh in enumerate((2 * p, 2 * p + 1)):
            lgb = jnp.broadcast_to(lg[hh:hh + 1, :], (CHUNK, LANES))
            keep = lane_lo if side == 0 else jnp.logical_not(lane_lo)
            keep_rows = row_lo if side == 0 else jnp.logical_not(row_lo)
            q_h = jnp.where(keep, q_pair, 0.0)
            scores = _dot_nt(q_h.astype(BF16), k_pair_bf)
            dec = jnp.where(incl, jnp.exp(lgb * dist), 0.0)
            m = (scores * dec).astype(BF16)
            qe = (q_h * jnp.exp(lgb * cnt)).astype(BF16)
            v_h = rv_ref[0, :, hh * RET_V_DIM:(hh + 1) * RET_V_DIM].astype(BF16)
            y = _dot(jnp.concatenate([m, qe], axis=1), jnp.concatenate([v_h, h_bf], axis=0))
            yret_ref[0, 0, :, hh * RET_V_DIM:(hh + 1) * RET_V_DIM] = y.astype(yret_ref.dtype)
            kts.append(jnp.where(keep_rows, k_t * jnp.exp(lgb * rem_t), 0.0).astype(BF16))
            vs.append(v_h)
            es.append(jnp.exp(lgb * float(CHUNK)))
        upd = _dot(jnp.concatenate(kts, axis=1), jnp.concatenate(vs, axis=0))
        hr_ref[p] = h_pair * jnp.where(row_lo, es[0], es[1]) + upd


def _even_scan(xbc, dtp, proj, cos_t, sin_t, dt_bias, a_log, ret_decay, *, n_ctx, rq_blk, rk_blk, rv_blk):
    b, l, _ = xbc.shape
    nc = l // CHUNK
    cmap = functools.partial(_chunk_of, n_ctx=n_ctx, n_all=nc)
    out_spec = pl.BlockSpec((1, 1, CHUNK, SSD_INNER), lambda bi, d, t: (bi, d, cmap(d, t), 0))
    return pl.pallas_call(
        _even_scan_kernel,
        grid=(b, 2, nc),
        in_specs=[
            pl.BlockSpec((1, CHUNK, SSD_INNER), lambda bi, d, t: (bi, cmap(d, t), 0)),
            pl.BlockSpec((1, CHUNK, 2 * SSD_GROUPS * SSD_STATE), lambda bi, d, t: (bi, cmap(d, t), 2)),
            pl.BlockSpec((1, CHUNK, LANES), lambda bi, d, t: (bi, cmap(d, t), d)),
            pl.BlockSpec((1, CHUNK, RET_QK), lambda bi, d, t: (bi, cmap(d, t), rq_blk)),
            pl.BlockSpec((1, CHUNK, RET_QK), lambda bi, d, t: (bi, cmap(d, t), rk_blk)),
            pl.BlockSpec((1, CHUNK, RET_V), lambda bi, d, t: (bi, cmap(d, t), rv_blk)),
            pl.BlockSpec((CHUNK, LANES), lambda bi, d, t: (cmap(d, t), 0)),
            pl.BlockSpec((CHUNK, LANES), lambda bi, d, t: (cmap(d, t), 0)),
            pl.BlockSpec((1, 1, LANES), lambda bi, d, t: (d, 0, 0)),
            pl.BlockSpec((1, 1, LANES), lambda bi, d, t: (d, 0, 0)),
            pl.BlockSpec((1, RET_HEADS, LANES), lambda bi, d, t: (d, 0, 0)),
        ],
        out_specs=[out_spec, out_spec],
        out_shape=[jax.ShapeDtypeStruct((b, 2, l, SSD_INNER), BF16),
                   jax.ShapeDtypeStruct((b, 2, l, RET_V), BF16)],
        scratch_shapes=[pltpu.VMEM((SSD_HEADS // 2, SSD_STATE, LANES), F32),
                        pltpu.VMEM((RET_HEADS // 2, LANES, RET_V_DIM), F32)],
        compiler_params=_cparams(("parallel", "parallel", "arbitrary")),
        name="even_scan",
    )(xbc, xbc, dtp, proj, proj, proj, cos_t, sin_t, dt_bias, a_log, ret_decay)


def _even_out_kernel(ys0_ref, ys1_ref, yr0_ref, yr1_ref, xs_ref, z_ref, rg_ref, dsk_ref, nw_ref, o_ref):
    def f32(ref, *idx):
        return ref[idx].astype(F32)

    y = (f32(ys0_ref, 0, 0) + f32(ys1_ref, 0, 0) + dsk_ref[...] * f32(xs_ref, 0)) * _silu(f32(z_ref, 0))
    gsz = SSD_INNER // SSD_GROUPS
    for g in range(SSD_GROUPS):
        sl = slice(g * gsz, (g + 1) * gsz)
        o_ref[0, :, sl] = _rms(y[:, sl], nw_ref[:, sl]).astype(BF16)
    yr = f32(yr0_ref, 0, 0) + f32(yr1_ref, 0, 0)
    gate = _silu(f32(rg_ref, 0))
    for hh in range(RET_HEADS):
        sl = slice(hh * RET_V_DIM, (hh + 1) * RET_V_DIM)
        v = yr[:, sl]
        mu = jnp.mean(v, axis=-1, keepdims=True)
        cen = v - mu
        var = jnp.mean(cen * cen, axis=-1, keepdims=True)
        o_ref[0, :, SSD_INNER + hh * RET_V_DIM:SSD_INNER + (hh + 1) * RET_V_DIM] = (
            cen * lax.rsqrt(var + EPS) * gate[:, sl]).astype(BF16)


EVEN_OUT_TILE = ROW_TILE


def _even_out(y_ssd, y_ret, xbc, proj, d_skip, ssd_norm, *, z_blk, rg_blk):
    b, _, l, _ = y_ssd.shape
    tr = EVEN_OUT_TILE
    w = SSD_INNER
    dsk = jnp.repeat(d_skip, SSD_HEAD_DIM).reshape(1, w)

    def dir_spec(dd):
        return pl.BlockSpec((1, 1, tr, w), lambda bi, j: (bi, dd, j, 0))

    def col_spec(blk):
        return pl.BlockSpec((1, tr, w), lambda bi, j: (bi, j, blk))

    vec = pl.BlockSpec((1, w), lambda bi, j: (0, 0))
    return pl.pallas_call(
        _even_out_kernel,
        grid=(b, l // tr),
        in_specs=[dir_spec(0), dir_spec(1), dir_spec(0), dir_spec(1), col_spec(0), col_spec(z_blk),
                  col_spec(rg_blk), vec, vec],
        out_specs=pl.BlockSpec((1, tr, 2 * w), lambda bi, j: (bi, j, 0)),
        out_shape=jax.ShapeDtypeStruct((b, l, 2 * w), BF16),
        compiler_params=_cparams(("parallel", "parallel")),
        name="even_out",
    )(y_ssd, y_ssd, y_ret, y_ret, xbc, proj, proj, dsk, ssd_norm.reshape(1, w))


GDN_LEVELS = 7
GDN_GROUP = 16


def _bdot(a, b):
    return jnp.einsum("gmk,gkn->gmn", a, b, preferred_element_type=F32)


def _bdot_nt(a, b):
    return jnp.einsum("gmk,gnk->gmn", a, b, preferred_element_type=F32)


def _gdn_scan_kernel(q_ref, k_ref, v_ref, raw_ref, bias_ref, alog_ref, lvl_ref, o_ref, s_ref):
    d = pl.program_id(1)
    t = pl.program_id(2)

    @pl.when(t == 0)
    def _():
        s_ref[...] = jnp.zeros_like(s_ref)

    ii, jj, fwd, incl, strict = _scan_masks(d)
    strict_f = jnp.where(strict, 1.0, 0.0)
    eye = jnp.where(ii == jj, 1.0, 0.0)
    tri = jnp.where(incl, 1.0, 0.0).astype(BF16)

    raw = raw_ref[0]
    beta = jax.nn.sigmoid(raw)
    gl = -jnp.exp(alog_ref[0]) * _softplus(raw + bias_ref[0])
    cs = _dot_sel_left(tri, gl)
    tot = jnp.sum(gl, axis=0, keepdims=True)
    cs_t = cs.T
    wend_t = jnp.exp(tot - cs).T

    gsz = GDN_GROUP
    rep = GDN_V_HEADS // GDN_K_HEADS
    kper = gsz // rep
    hd = GDN_HEAD_DIM
    strict_f3 = strict_f[None]
    incl3 = incl[None]

    def col_bcast(a, c):
        return jnp.broadcast_to(a[:, c:c + 1], (CHUNK, LANES))

    def per_key_head(a4, x):
        res = a4[:, None] * x.reshape((kper, rep) + x.shape[1:])
        return res.reshape((gsz,) + res.shape[2:])

    for gi in range(GDN_V_HEADS // gsz):
        heads = range(gi * gsz, (gi + 1) * gsz)
        kheads = range(gi * kper, (gi + 1) * kper)
        q4b = jnp.stack([q_ref[0, :, kh * hd:(kh + 1) * hd] for kh in kheads])
        k4b = jnp.stack([k_ref[0, :, kh * hd:(kh + 1) * hd] for kh in kheads])
        k4 = k4b.astype(F32)
        kk4 = _bdot_nt(k4b, k4b)
        qk4 = _bdot_nt(q4b, k4b)
        kt4 = jnp.stack([k4[i].T for i in range(kper)])
        v = jnp.stack([v_ref[0, :, h * hd:(h + 1) * hd] for h in heads])
        cs_b = jnp.stack([col_bcast(cs, GDN_V_HEADS + h) for h in heads])
        beta_b = jnp.stack([col_bcast(beta, h) for h in heads])
        cs_r = jnp.stack([cs_t[GDN_V_HEADS + h:GDN_V_HEADS + h + 1, :] for h in heads])
        wend_r = jnp.stack([wend_t[GDN_V_HEADS + h:GDN_V_HEADS + h + 1, :] for h in heads])
        gam = jnp.exp(jnp.where(incl3, cs_b - cs_r, NEG_BIG))
        ecs_b = jnp.exp(cs_b)
        etot = jnp.exp(jnp.where(fwd, cs_b[:, CHUNK - 1:CHUNK, :], cs_b[:, 0:1, :]))
        low = per_key_head(kk4, gam * (strict_f3 * beta_b))
        tinv = eye[None] - low * lvl_ref[0][None]
        for lv in range(1, GDN_LEVELS):
            a_off = (low * lvl_ref[lv][None]).astype(BF16)
            x = _bdot(a_off, tinv.astype(BF16))
            tinv = tinv - _bdot(tinv.astype(BF16), x.astype(BF16))
        state = s_ref[gi * gsz:(gi + 1) * gsz]
        ke = per_key_head(k4, ecs_b).astype(BF16)
        qe = per_key_head(q4b.astype(F32), ecs_b).astype(BF16)
        kq_s = _bdot(jnp.concatenate([ke, qe], axis=1), state.astype(BF16))
        resid = (beta_b * (v.astype(F32) - kq_s[:, :CHUNK])).astype(BF16)
        v_new_bf = _bdot(tinv.astype(BF16), resid).astype(BF16)
        out = kq_s[:, CHUNK:] + _bdot(per_key_head(qk4, gam).astype(BF16), v_new_bf)
        for g, h in enumerate(heads):
            o_ref[0, 0, :, h * hd:(h + 1) * hd] = out[g].astype(o_ref.dtype)
        k_end_t = per_key_head(kt4, wend_r).astype(BF16)
        s_ref[gi * gsz:(gi + 1) * gsz] = state * etot + _bdot(k_end_t, v_new_bf)


def _gdn_level_masks():
    i = jnp.arange(CHUNK)[:, None]
    j = jnp.arange(CHUNK)[None, :]
    masks = []
    for lv in range(GDN_LEVELS):
        blk = 1 << lv
        masks.append(((i // (2 * blk) == j // (2 * blk)) & (i // blk != j // blk)).astype(F32))
    return jnp.stack(masks)


def _gdn_scan(qkv, raw, bias, a_log, *, n_ctx):
    b, l, _ = qkv.shape
    nc = l // CHUNK
    cmap = functools.partial(_chunk_of, n_ctx=n_ctx, n_all=nc)
    return pl.pallas_call(
        _gdn_scan_kernel,
        grid=(b, 2, nc),
        in_specs=[
            pl.BlockSpec((1, CHUNK, GDN_K), lambda bi, d, t: (bi, cmap(d, t), 0)),
            pl.BlockSpec((1, CHUNK, GDN_K), lambda bi, d, t: (bi, cmap(d, t), 1)),
            pl.BlockSpec((1, CHUNK, GDN_V), lambda bi, d, t: (bi, cmap(d, t), 1)),
            pl.BlockSpec((1, CHUNK, LANES), lambda bi, d, t: (bi, cmap(d, t), d)),
            pl.BlockSpec((1, 1, LANES), lambda bi, d, t: (d, 0, 0)),
            pl.BlockSpec((1, 1, LANES), lambda bi, d, t: (d, 0, 0)),
            pl.BlockSpec((GDN_LEVELS, CHUNK, CHUNK), lambda bi, d, t: (0, 0, 0)),
        ],
        out_specs=pl.BlockSpec((1, 1, CHUNK, GDN_V), lambda bi, d, t: (bi, d, cmap(d, t), 0)),
        out_shape=jax.ShapeDtypeStruct((b, 2, l, GDN_V), BF16),
        scratch_shapes=[pltpu.VMEM((GDN_V_HEADS, GDN_HEAD_DIM, GDN_HEAD_DIM), F32)],
        compiler_params=_cparams(("parallel", "parallel", "arbitrary")),
        name="gdn_scan",
    )(qkv, qkv, qkv, raw, bias, a_log, _gdn_level_masks())


def _odd_out_kernel(o0_ref, o1_ref, z_ref, nw_ref, out_ref):
    o = o0_ref[0, 0].astype(F32) + o1_ref[0, 0].astype(F32)
    gate = _silu(z_ref[0].astype(F32))
    for hh in range(GDN_V_HEADS):
        sl = slice(hh * GDN_HEAD_DIM, (hh + 1) * GDN_HEAD_DIM)
        out_ref[0, :, sl] = (_rms(o[:, sl], nw_ref[...]) * gate[:, sl]).astype(BF16)


def _odd_out(o, proj, norm_w, *, z_blk, row_off):
    b, _, l, w = o.shape
    tr = ROW_TILE
    n_tiles = l // tr - row_off

    def dir_spec(dd):
        return pl.BlockSpec((1, 1, tr, w), lambda bi, j: (bi, dd, j + row_off, 0))

    return pl.pallas_call(
        _odd_out_kernel,
        grid=(b, n_tiles),
        in_specs=[dir_spec(0), dir_spec(1),
                  pl.BlockSpec((1, tr, w), lambda bi, j: (bi, j + row_off, z_blk)),
                  pl.BlockSpec((1, GDN_HEAD_DIM), lambda bi, j: (0, 0))],
        out_specs=pl.BlockSpec((1, tr, w), lambda bi, j: (bi, j, 0)),
        out_shape=jax.ShapeDtypeStruct((b, n_tiles * tr, w), BF16),
        compiler_params=_cparams(("parallel", "parallel")),
        name="odd_out",
    )(o, o, proj, norm_w.reshape(1, GDN_HEAD_DIM))


def _dir_lanes(p, lane_off):
    out = jnp.zeros((2, 1, LANES), F32)
    return out.at[:, 0, lane_off:lane_off + p.shape[1]].set(p.astype(F32))


def _rotary_tables(n_ctx_rows, n_lat_rows):
    nf = RET_QK_DIM // 4
    pos = jnp.arange(n_lat_rows)
    rowp = (pos // GRID_W).astype(F32)
    colp = (pos % GRID_W).astype(F32)
    freqs = ROPE_BASE ** (-jnp.arange(nf, dtype=F32) / nf)
    ang = jnp.concatenate([rowp[:, None] * freqs, colp[:, None] * freqs], axis=-1)
    cos = jnp.cos(ang)
    sin = jnp.sin(ang)
    cos_h = jnp.concatenate([cos, cos], axis=-1)
    sin_h = jnp.concatenate([-sin, sin], axis=-1)
    reps = LANES // RET_QK_DIM
    cos_t = jnp.concatenate([jnp.ones((n_ctx_rows, LANES), F32), jnp.tile(cos_h, (1, reps))], axis=0)
    sin_t = jnp.concatenate([jnp.zeros((n_ctx_rows, LANES), F32), jnp.tile(sin_h, (1, reps))], axis=0)
    return cos_t, sin_t


EV_Z, EV_RG, EV_RV, EV_XBC, EV_RQ, EV_RK = 0, 2048, 4096, 6144, 9216, 10240
EV_MAIN = 11264
SSD_CONV_DIM = SSD_INNER + 2 * SSD_GROUPS * SSD_STATE


def _even_weights(w_in):
    o = 0
    parts = {}
    for name, sz in (("z", SSD_INNER), ("xbc", SSD_CONV_DIM), ("dt", 2 * SSD_HEADS), ("rq", RET_QK),
                     ("rk", RET_QK), ("rv", RET_V), ("rg", RET_V)):
        parts[name] = w_in[:, o:o + sz]
        o += sz
    main = jnp.concatenate([parts[nm] for nm in ("z", "rg", "rv", "xbc", "rq", "rk")], axis=1).astype(BF16)
    pad = jnp.zeros((w_in.shape[0], LANES - SSD_HEADS), w_in.dtype)
    dt = parts["dt"]
    w_dt = jnp.concatenate([dt[:, :SSD_HEADS], pad, dt[:, SSD_HEADS:], pad], axis=1)
    return main[None], w_dt[None]


def _odd_small_weights(w_in):
    base = 2 * GDN_K + 2 * GDN_V
    hv = GDN_V_HEADS
    beta = w_in[:, base:base + 2 * hv]
    dec = w_in[:, base + 2 * hv:base + 4 * hv]
    pad = jnp.zeros((w_in.shape[0], LANES - 2 * hv), w_in.dtype)
    return jnp.concatenate([beta[:, :hv], dec[:, :hv], pad, beta[:, hv:], dec[:, hv:], pad], axis=1)[None]


def _ffn(s, o, mod, layer, post_mix, pre_ffn, w1, w3, w2, *, s_off=0):
    b, l, d = o.shape
    s1, h = _res_norm(s, mod, o=o, gate=(layer, 2), post_w=post_mix, pre_w=pre_ffn,
                      shift=(layer, 3), scale=(layer, 4), s_off=s_off, n_tiles=l // ROW_TILE,
                      has_ctx=s_off == 0)
    d_ff = w1.shape[2]
    act = _matmul(h.reshape(b * l, d), [w1, w3], layer=layer, n=d_ff, tn=256, out_dtype=BF16)
    f = _matmul_ktiled(act, w2, layer=layer)
    return s1, f.reshape(b, l, d)


def kernel(x, c, ctx, c_ctx, ada_w, ada_b, norm_mix_pre, norm_mix_post, norm_ffn_pre, norm_ffn_post,
           ev_w_in, ev_conv_w, ev_conv_b, ev_dt_bias, ev_a_log, ev_d_skip, ev_ssd_norm, ev_ret_decay,
           ev_w_out, od_w_in, od_conv_w, od_dt_bias, od_a_log, od_norm, od_w_out,
           ffn_w1, ffn_w3, ffn_w2):
    b, seq, d = x.shape
    n_ctx_rows = ctx.shape[1]
    assert n_ctx_rows == ROW_TILE and seq % ROW_TILE == 0
    l = n_ctx_rows + seq
    m = b * l
    n_ctx = n_ctx_rows // CHUNK

    mod = _ada(jnp.concatenate([c, c_ctx[None]], axis=0), ada_w, ada_b)

    s, h = _res_norm((ctx, x), mod, pre_w=norm_mix_pre[0], shift=(0, 0), scale=(0, 1))
    w_main, w_dt = _even_weights(ev_w_in[0])
    h2 = h.reshape(m, d)
    proj = _matmul(h2, [w_main], n=EV_MAIN, tn=512, out_dtype=BF16).reshape(b, l, EV_MAIN)
    dtp = _matmul(h2, [w_dt], n=2 * LANES, tn=2 * LANES, out_dtype=F32).reshape(b, l, 2 * LANES)
    xbc = _conv_silu(proj, ev_conv_w[0], ev_conv_b[0], col_off=EV_XBC, width=SSD_CONV_DIM)
    cos_t, sin_t = _rotary_tables(n_ctx_rows, seq)
    rdec = jnp.broadcast_to(ev_ret_decay[0][:, :, None], (2, RET_HEADS, LANES)).astype(F32)
    y_ssd, y_ret = _even_scan(xbc, dtp, proj, cos_t, sin_t, _dir_lanes(ev_dt_bias[0], 0),
                              _dir_lanes(ev_a_log[0], 0), rdec, n_ctx=n_ctx,
                              rq_blk=EV_RQ // RET_QK, rk_blk=EV_RK // RET_QK, rv_blk=EV_RV // RET_V)
    act = _even_out(y_ssd, y_ret, xbc, proj, ev_d_skip[0], ev_ssd_norm[0],
                    z_blk=EV_Z // SSD_INNER, rg_blk=EV_RG // SSD_INNER)
    o = _matmul(act.reshape(m, act.shape[-1]), [ev_w_out], layer=0, n=d, tn=512, out_dtype=BF16).reshape(b, l, d)
    s, f = _ffn(s, o, mod, 0, norm_mix_post[0], norm_ffn_pre[0], ffn_w1, ffn_w3, ffn_w2)

    s, h = _res_norm(s, mod, o=f, gate=(0, 5), post_w=norm_ffn_post[0], pre_w=norm_mix_pre[1],
                     shift=(1, 0), scale=(1, 1))
    h2 = h.reshape(m, d)
    n_main = 2 * GDN_K + 2 * GDN_V
    proj = _matmul(h2, [od_w_in], layer=0, n=n_main, tn=512, out_dtype=BF16).reshape(b, l, n_main)
    raw = _matmul(h2, [_odd_small_weights(od_w_in[0])], n=2 * LANES, tn=2 * LANES,
                  out_dtype=F32).reshape(b, l, 2 * LANES)
    qkv = _conv_silu(proj, od_conv_w[0], None, col_off=0, width=2 * GDN_K + GDN_V,
                     l2_blocks=2 * GDN_K // CONV_CW, l2_scaled_blocks=GDN_K // CONV_CW,
                     l2_scale=GDN_HEAD_DIM ** -0.5)
    o2 = _gdn_scan(qkv, raw, _dir_lanes(od_dt_bias[0], GDN_V_HEADS), _dir_lanes(od_a_log[0], GDN_V_HEADS),
                   n_ctx=n_ctx)
    ctx_tiles = n_ctx_rows // ROW_TILE
    act = _odd_out(o2, proj, od_norm[0], z_blk=(2 * GDN_K + GDN_V) // GDN_V, row_off=ctx_tiles)
    o = _matmul(act.reshape(b * seq, act.shape[-1]), [od_w_out], layer=0, n=d, tn=512,
                out_dtype=BF16).reshape(b, seq, d)
    s, f = _ffn(s, o, mod, 1, norm_mix_post[1], norm_ffn_pre[1], ffn_w1, ffn_w3, ffn_w2, s_off=ctx_tiles)
    out, _ = _res_norm(s, mod, o=f, gate=(1, 5), post_w=norm_ffn_post[1], has_ctx=False)
    return out
```

```python
import functools
import math

import jax
import jax.numpy as jnp
from jax import lax
from jax.experimental import pallas as pl
from jax.experimental.pallas import tpu as pltpu

F32 = jnp.float32
BF16 = jnp.bfloat16

EPS = 1e-6
CHUNK = 128
LANES = 128
CONV_W = 5
CONV_HALO = 16
GRID_W = 64
ROPE_BASE = 10000.0
ROW_TILE = 256
NEG_BIG = -1e30

SSD_HEADS = 32
SSD_HEAD_DIM = 64
SSD_GROUPS = 4
SSD_STATE = 128
SSD_INNER = SSD_HEADS * SSD_HEAD_DIM
RET_HEADS = 16
RET_QK_DIM = 64
RET_V_DIM = 128
RET_QK = RET_HEADS * RET_QK_DIM
RET_V = RET_HEADS * RET_V_DIM
GDN_K_HEADS = 16
GDN_V_HEADS = 32
GDN_HEAD_DIM = 128
GDN_K = GDN_K_HEADS * GDN_HEAD_DIM
GDN_V = GDN_V_HEADS * GDN_HEAD_DIM

VMEM_LIMIT = 56 * 1024 * 1024


def _cparams(sem):
    return pltpu.CompilerParams(dimension_semantics=sem, vmem_limit_bytes=VMEM_LIMIT)


def _dot(a, b):
    return jnp.dot(a, b, preferred_element_type=F32)


def _dot_nt(a, b):
    return lax.dot_general(a, b, (((1,), (1,)), ((), ())), preferred_element_type=F32)


def _split3(x):
    hi = x.astype(BF16)
    r1 = x - hi.astype(F32)
    mid = r1.astype(BF16)
    lo = (r1 - mid.astype(F32)).astype(BF16)
    return hi, mid, lo


def _dot_sel_left(sel, x):
    hi, mid, lo = _split3(x)
    return _dot(sel, hi) + _dot(sel, mid) + _dot(sel, lo)


def _dot_sel_right(x, sel):
    hi, mid, lo = _split3(x)
    return _dot(hi, sel) + _dot(mid, sel) + _dot(lo, sel)


def _silu(x):
    return x * jax.nn.sigmoid(x)


def _softplus(x):
    return jnp.maximum(x, 0.0) + jnp.log(1.0 + jnp.exp(-jnp.abs(x)))


def _rms(x, w):
    ms = jnp.mean(x * x, axis=-1, keepdims=True)
    return x * lax.rsqrt(ms + EPS) * w


ADA_TN = 512


def _ada_kernel(cb_ref, w_ref, b_ref, o_ref, act_ref, *, n_rows):
    first = jnp.logical_and(pl.program_id(0) == 0, pl.program_id(1) == 0)

    @pl.when(first)
    def _():
        act_ref[...] = _silu(cb_ref[...])

    d = w_ref.shape[1]
    tn = w_ref.shape[2]
    w3 = w_ref[0].reshape(d // 8, 8, tn)
    o_ref[...] = jnp.zeros_like(o_ref)
    for r in range(n_rows):
        a3 = act_ref[r].reshape(d // 8, 8, LANES)
        for lt in range(tn // LANES):
            sl = slice(lt * LANES, (lt + 1) * LANES)
            part = jnp.sum(w3[:, :, sl] * a3, axis=0)
            o_ref[0, r:r + 1, sl] = jnp.sum(part, axis=0, keepdims=True) + b_ref[0, :, sl]


def _ada(cvec, ada_w, ada_b):
    n_rows, d = cvec.shape
    depth, _, n = ada_w.shape
    cb = jnp.broadcast_to(cvec[:, :, None], (n_rows, d, LANES))
    return pl.pallas_call(
        functools.partial(_ada_kernel, n_rows=n_rows),
        grid=(depth, n // ADA_TN),
        in_specs=[
            pl.BlockSpec((n_rows, d, LANES), lambda i, j: (0, 0, 0)),
            pl.BlockSpec((1, d, ADA_TN), lambda i, j: (i, 0, j)),
            pl.BlockSpec((1, 1, ADA_TN), lambda i, j: (i, 0, j)),
        ],
        out_specs=pl.BlockSpec((1, 8, ADA_TN), lambda i, j: (i, 0, j)),
        out_shape=jax.ShapeDtypeStruct((depth, 8, n), F32),
        scratch_shapes=[pltpu.VMEM((n_rows, d, LANES), F32)],
        compiler_params=_cparams(("arbitrary", "arbitrary")),
        name="ada_mod",
    )(cb, ada_w, ada_b.reshape(depth, 1, n))


def _mod_row(m_ref, ctx_row):
    r = pl.program_id(0)
    if ctx_row is not None:
        r = jnp.where(pl.program_id(1) == 0, ctx_row, r)
    return m_ref[0, pl.ds(r, 1), :]


def _res_norm_kernel(*refs, split_in, has_res, has_h, ctx_row):
    it = iter(refs)
    if split_in:
        ctx_ref, x_ref = next(it), next(it)
        s = jnp.where(pl.program_id(1) == 0, ctx_ref[0], x_ref[0])
    else:
        s = next(it)[0]
    if has_res:
        o_ref, gate_ref, post_ref = next(it), next(it), next(it)
    if has_h:
        pre_ref, shift_ref, scale_ref = next(it), next(it), next(it)
    if has_res:
        s = s + _mod_row(gate_ref, ctx_row) * _rms(o_ref[0].astype(F32), post_ref[...])
    if has_res or split_in:
        next(it)[0] = s
    if has_h:
        h = _rms(s, pre_ref[...]) * (1.0 + _mod_row(scale_ref, ctx_row)) + _mod_row(shift_ref, ctx_row)
        next(it)[0] = h.astype(BF16)


def _res_norm(s, mod, *, o=None, gate=None, post_w=None, pre_w=None, shift=None, scale=None,
              s_off=0, o_off=0, n_tiles=None, has_ctx=True):
    split_in = isinstance(s, tuple)
    b, _, d = (s[1] if split_in else s).shape
    has_res = o is not None
    has_h = pre_w is not None
    if n_tiles is None:
        n_tiles = (s[0].shape[1] + s[1].shape[1]) // ROW_TILE if split_in else s.shape[1] // ROW_TILE - s_off
    ctx_row = b if has_ctx else None

    def outrows(bi, j):
        return (bi, j, 0)

    def mod_spec(sel):
        layer, comp = sel
        return pl.BlockSpec((1, 8, d), lambda bi, j: (layer, 0, comp))

    vec_spec = pl.BlockSpec((1, d), lambda bi, j: (0, 0))
    tile = (1, ROW_TILE, d)
    if split_in:
        assert s[0].shape[1] == ROW_TILE and has_ctx and not has_res
        args = list(s)
        in_specs = [pl.BlockSpec(tile, lambda bi, j: (bi, 0, 0)),
                    pl.BlockSpec(tile, lambda bi, j: (bi, jnp.maximum(j - 1, 0), 0))]
    else:
        args, in_specs = [s], [pl.BlockSpec(tile, lambda bi, j: (bi, j + s_off, 0))]
    out_specs, out_shape = [], []
    if has_res:
        args += [o, mod, post_w.reshape(1, d)]
        in_specs += [pl.BlockSpec(tile, lambda bi, j: (bi, j + o_off, 0)), mod_spec(gate), vec_spec]
    if has_res or split_in:
        out_specs.append(pl.BlockSpec(tile, outrows))
        out_shape.append(jax.ShapeDtypeStruct((b, n_tiles * ROW_TILE, d), F32))
    if has_h:
        args += [pre_w.reshape(1, d), mod, mod]
        in_specs += [vec_spec, mod_spec(shift), mod_spec(scale)]
        out_specs.append(pl.BlockSpec(tile, outrows))
        out_shape.append(jax.ShapeDtypeStruct((b, n_tiles * ROW_TILE, d), BF16))
    outs = pl.pallas_call(
        functools.partial(_res_norm_kernel, split_in=split_in, has_res=has_res, has_h=has_h, ctx_row=ctx_row),
        grid=(b, n_tiles),
        in_specs=in_specs,
        out_specs=out_specs,
        out_shape=out_shape,
        compiler_params=_cparams(("parallel", "parallel")),
        name="res_norm",
    )(*args)
    outs = list(outs)
    s_new = outs.pop(0) if (has_res or split_in) else None
    h = outs.pop(0) if has_h else None
    return s_new, h


MM_TM = 2048


def _row_tile(m, cap=MM_TM):
    best = ROW_TILE
    for tm in range(ROW_TILE, cap + 1, ROW_TILE):
        if m % tm == 0:
            best = tm
    return best


def _mm_kernel(a_ref, *refs, swiglu):
    a = a_ref[...]
    if swiglu:
        w1_ref, w3_ref, o_ref = refs
        g = _dot(a, w1_ref[0].astype(BF16))
        u = _dot(a, w3_ref[0].astype(BF16))
        o_ref[...] = (_silu(g) * u).astype(o_ref.dtype)
    else:
        w_ref, o_ref = refs
        o_ref[...] = _dot(a, w_ref[0].astype(BF16)).astype(o_ref.dtype)


def _matmul(a, ws, *, layer=0, col_off=0, n, tn, out_dtype):
    m, k = a.shape
    swiglu = len(ws) == 2
    tm = _row_tile(m)
    assert col_off % tn == 0
    off = col_off // tn
    w_spec = pl.BlockSpec((1, k, tn), lambda i, j: (layer, 0, j + off))
    return pl.pallas_call(
        functools.partial(_mm_kernel, swiglu=swiglu),
        grid=(m // tm, pl.cdiv(n, tn)),
        in_specs=[pl.BlockSpec((tm, k), lambda i, j: (i, 0), pipeline_mode=pl.Buffered(1))]
        + [w_spec] * len(ws),
        out_specs=pl.BlockSpec((tm, tn), lambda i, j: (i, j)),
        out_shape=jax.ShapeDtypeStruct((m, n), out_dtype),
        compiler_params=_cparams(("parallel", "arbitrary")),
        name="matmul_swiglu" if swiglu else "matmul",
    )(a, *ws)


MMK_TN = 1024
MMK_TK = 1024


def _mm_ktiled_kernel(a_ref, w_ref, o_ref, acc_ref, *, k_total, tk):
    kk = pl.program_id(2)
    nk = pl.num_programs(2)

    @pl.when(kk == 0)
    def _():
        acc_ref[...] = jnp.zeros_like(acc_ref)

    rem = k_total - (pl.cdiv(k_total, tk) - 1) * tk

    def step(a, w):
        acc_ref[...] += _dot(a, w.astype(BF16))

    if rem == tk:
        step(a_ref[...], w_ref[0])
    else:
        @pl.when(kk < nk - 1)
        def _():
            step(a_ref[...], w_ref[0])

        @pl.when(kk == nk - 1)
        def _():
            a = a_ref[...]
            w = w_ref[0]
            ka = lax.broadcasted_iota(jnp.int32, a.shape, 1)
            kw = lax.broadcasted_iota(jnp.int32, w.shape, 0)
            step(jnp.where(ka < rem, a, jnp.zeros_like(a)), jnp.where(kw < rem, w, jnp.zeros_like(w)))

    @pl.when(kk == nk - 1)
    def _():
        o_ref[...] = acc_ref[...].astype(o_ref.dtype)


def _matmul_ktiled(a, w, *, layer, tk=MMK_TK):
    m, k = a.shape
    n = w.shape[2]
    tm = _row_tile(m)
    tn = min(MMK_TN, n)
    assert n % tn == 0
    return pl.pallas_call(
        functools.partial(_mm_ktiled_kernel, k_total=k, tk=tk),
        grid=(m // tm, n // tn, pl.cdiv(k, tk)),
        in_specs=[pl.BlockSpec((tm, tk), lambda i, j, kk: (i, kk)),
                  pl.BlockSpec((1, tk, tn), lambda i, j, kk: (layer, kk, j))],
        out_specs=pl.BlockSpec((tm, tn), lambda i, j, kk: (i, j)),
        out_shape=jax.ShapeDtypeStruct((m, n), BF16),
        scratch_shapes=[pltpu.VMEM((tm, tn), F32)],
        compiler_params=_cparams(("parallel", "parallel", "arbitrary")),
        name="matmul_ktiled",
    )(a, w)


CONV_CW = 1024


def _conv_kernel(cur_ref, prev_ref, next_ref, w_ref, *refs, has_bias, l2_scaled_blocks, l2_scale):
    if has_bias:
        b_ref, o_ref = refs
    else:
        (o_ref,) = refs
    j = pl.program_id(1)
    nj = pl.num_programs(1)
    tr = cur_ref.shape[1]
    use_prev = jnp.logical_and(j != 0, j != 1)
    use_next = jnp.logical_and(j != 0, j != nj - 1)
    zeros = jnp.zeros(prev_ref.shape[1:], F32)
    xext = jnp.concatenate([jnp.where(use_prev, prev_ref[0].astype(F32), zeros),
                            cur_ref[0].astype(F32),
                            jnp.where(use_next, next_ref[0].astype(F32), zeros)], axis=0)
    n_ext = xext.shape[0]
    acc = None
    for t in range(CONV_W):
        shift = (CONV_W // 2 - t) % n_ext
        tap = pltpu.roll(xext, shift, axis=0) if shift else xext
        term = w_ref[t:t + 1, :] * tap[CONV_HALO:CONV_HALO + tr, :]
        acc = term if acc is None else acc + term
    if has_bias:
        acc = acc + b_ref[...]
    y = _silu(acc)
    if l2_scaled_blocks is None:
        o_ref[0] = y.astype(o_ref.dtype)
    else:
        scale = jnp.where(pl.program_id(2) < l2_scaled_blocks, l2_scale, 1.0).astype(F32)
        for hd in range(y.shape[1] // LANES):
            sl = slice(hd * LANES, (hd + 1) * LANES)
            yh = y[:, sl]
            ss = jnp.sum(yh * yh, axis=-1, keepdims=True)
            o_ref[0, :, sl] = (yh * (lax.rsqrt(ss + EPS) * scale)).astype(o_ref.dtype)


def _conv_silu(proj, conv_w, conv_b, *, col_off, width, l2_scaled_blocks=None, l2_scale=1.0):
    b, l, _ = proj.shape
    assert col_off % CONV_CW == 0 and width % CONV_CW == 0 and l % ROW_TILE == 0
    coff = col_off // CONV_CW
    hpt = ROW_TILE // CONV_HALO
    n_halo = l // CONV_HALO
    has_bias = conv_b is not None
    args = [proj, proj, proj, conv_w]
    in_specs = [
        pl.BlockSpec((1, ROW_TILE, CONV_CW), lambda bi, j, c: (bi, j, c + coff)),
        pl.BlockSpec((1, CONV_HALO, CONV_CW), lambda bi, j, c: (bi, jnp.maximum(j * hpt - 1, 0), c + coff)),
        pl.BlockSpec((1, CONV_HALO, CONV_CW),
                     lambda bi, j, c: (bi, jnp.minimum((j + 1) * hpt, n_halo - 1), c + coff)),
        pl.BlockSpec((CONV_W, CONV_CW), lambda bi, j, c: (0, c)),
    ]
    if has_bias:
        args.append(conv_b.reshape(1, width))
        in_specs.append(pl.BlockSpec((1, CONV_CW), lambda bi, j, c: (0, c)))
    return pl.pallas_call(
        functools.partial(_conv_kernel, has_bias=has_bias, l2_scaled_blocks=l2_scaled_blocks, l2_scale=l2_scale),
        grid=(b, l // ROW_TILE, width // CONV_CW),
        in_specs=in_specs,
        out_specs=pl.BlockSpec((1, ROW_TILE, CONV_CW), lambda bi, j, c: (bi, j, c)),
        out_shape=jax.ShapeDtypeStruct((b, l, width), BF16),
        compiler_params=_cparams(("parallel", "parallel", "parallel")),
        name="conv_silu",
    )(*args)


def _chunk_of(d, t, n_ctx, n_all):
    rev = jnp.where(t < n_ctx, n_ctx - 1 - t, n_all - 1 - (t - n_ctx))
    return jnp.where(d == 0, t, rev)


def _scan_masks(d, reps=1):
    ii = lax.broadcasted_iota(jnp.int32, (CHUNK, reps * CHUNK), 0)
    jj = lax.broadcasted_iota(jnp.int32, (CHUNK, reps * CHUNK), 1) % CHUNK
    fwd = d == 0
    later = jnp.where(fwd, ii, jj)
    earlier = jnp.where(fwd, jj, ii)
    incl = earlier <= later
    strict = earlier < later
    return ii, jj, fwd, incl, strict


def _even_scan_kernel(xs_ref, bc_ref, dt_ref, rq_ref, rk_ref, rv_ref, cos_ref, sin_ref,
                      dtb_ref, alog_ref, rdec_ref, yssd_ref, yret_ref, hs_ref, hr_ref):
    d = pl.program_id(1)
    t = pl.program_id(2)

    @pl.when(t == 0)
    def _():
        hs_ref[...] = jnp.zeros_like(hs_ref)
        hr_ref[...] = jnp.zeros_like(hr_ref)

    ii, jj, fwd, incl, _ = _scan_masks(d)
    lane = lax.broadcasted_iota(jnp.int32, (CHUNK, LANES), 1)
    row = lax.broadcasted_iota(jnp.int32, (CHUNK, LANES), 0)
    lane_lo = lane < (LANES // 2)
    row_lo = row < (CHUNK // 2)

    dtf = _softplus(dt_ref[0] + dtb_ref[0])
    la = -jnp.exp(alog_ref[0]) * dtf
    tri = jnp.where(incl, 1.0, 0.0).astype(BF16)
    cs = _dot_sel_left(tri, la)
    tot = jnp.sum(la, axis=0, keepdims=True)
    cs_t = cs.T
    dt_t = dtf.T
    wend_t = (dtf * jnp.exp(tot - cs)).T

    bc = bc_ref[0].astype(F32)
    n_state = SSD_GROUPS * SSD_STATE
    for g in range(SSD_GROUPS):
        bm = bc[:, g * SSD_STATE:(g + 1) * SSD_STATE]
        cm = bc[:, n_state + g * SSD_STATE:n_state + (g + 1) * SSD_STATE]
        bm_bf = bm.astype(BF16)
        cm_bf = cm.astype(BF16)
        scores = _dot_nt(cm_bf, bm_bf)
        bm_t = bm.T
        heads_per_group = SSD_HEADS // SSD_GROUPS
        for pp in range(heads_per_group // 2):
            pidx = g * (heads_per_group // 2) + pp
            xs_pair = xs_ref[0, :, pidx * LANES:(pidx + 1) * LANES]
            x_lo = jnp.where(lane_lo, xs_pair, 0.0).astype(BF16)
            x_hi = jnp.where(lane_lo, 0.0, xs_pair).astype(BF16)
            h_pair = hs_ref[pidx]
            h_lo = jnp.where(lane_lo, h_pair, 0.0).astype(BF16)
            h_hi = jnp.where(lane_lo, 0.0, h_pair).astype(BF16)
            ms, qs, kws, ets = [], [], [], []
            for r in (2 * pidx, 2 * pidx + 1):
                cs_b = jnp.broadcast_to(cs[:, r:r + 1], (CHUNK, LANES))
                dec = jnp.exp(jnp.where(incl, cs_b - cs_t[r:r + 1, :], NEG_BIG))
                ms.append((scores * dec * dt_t[r:r + 1, :]).astype(BF16))
                qs.append((cm * jnp.exp(cs_b)).astype(BF16))
                kws.append((bm_t * wend_t[r:r + 1, :]).astype(BF16))
                ets.append(jnp.exp(jnp.where(fwd, cs_b[CHUNK - 1:CHUNK, :], cs_b[0:1, :])))
            lhs = jnp.concatenate(ms + qs, axis=1)
            rhs = jnp.concatenate([x_lo, x_hi, h_lo, h_hi], axis=0)
            yssd_ref[0, 0, :, pidx * LANES:(pidx + 1) * LANES] = _dot(lhs, rhs).astype(yssd_ref.dtype)
            et = jnp.where(lane_lo, jnp.broadcast_to(ets[0], (CHUNK, LANES)),
                           jnp.broadcast_to(ets[1], (CHUNK, LANES)))
            upd = _dot(jnp.concatenate(kws, axis=1), jnp.concatenate([x_lo, x_hi], axis=0))
            hs_ref[pidx] = h_pair * et + upd

    lg = jax.nn.log_sigmoid(rdec_ref[0])
    dist = jnp.where(fwd, ii - jj, jj - ii).astype(F32)
    cnt = jnp.where(fwd, ii + 1, CHUNK - ii).astype(F32)
    rem_t = jnp.where(fwd, CHUNK - 1 - jj, jj).astype(F32)
    cos = cos_ref[...]
    sin = sin_ref[...]
    half = RET_QK_DIM // 2
    swap_lo = (lane % RET_QK_DIM) < half

    def rotary(tt):
        swapped = jnp.where(swap_lo, pltpu.roll(tt, LANES - half, axis=1), pltpu.roll(tt, half, axis=1))
        return tt * cos + swapped * sin

    for p in range(RET_HEADS // 2):
        sl = slice(p * LANES, (p + 1) * LANES)
        q_pair = rotary(rq_ref[0, :, sl].astype(F32))
        k_pair = rotary(rk_ref[0, :, sl].astype(F32) * (RET_QK_DIM ** -0.5))
        k_pair_bf = k_pair.astype(BF16)
        k_t = k_pair.T
        h_pair = hr_ref[p]
        h_bf = h_pair.astype(BF16)
        kts, vs, es = [], [], []
        for side, hh in enumerate((2 * p, 2 * p + 1)):
            lgb = jnp.broadcast_to(lg[hh:hh + 1, :], (CHUNK, LANES))
            keep = lane_lo if side == 0 else jnp.logical_not(lane_lo)
            keep_rows = row_lo if side == 0 else jnp.logical_not(row_lo)
            q_h = jnp.where(keep, q_pair, 0.0)
            scores = _dot_nt(q_h.astype(BF16), k_pair_bf)
            dec = jnp.where(incl, jnp.exp(lgb * dist), 0.0)
            m = (scores * dec).astype(BF16)
            qe = (q_h * jnp.exp(lgb * cnt)).astype(BF16)
            v_h = rv_ref[0, :, hh * RET_V_DIM:(hh + 1) * RET_V_DIM].astype(BF16)
            y = _dot(jnp.concatenate([m, qe], axis=1), jnp.concatenate([v_h, h_bf], axis=0))
            yret_ref[0, 0, :, hh * RET_V_DIM:(hh + 1) * RET_V_DIM] = y.astype(yret_ref.dtype)
            kts.append(jnp.where(keep_rows, k_t * jnp.exp(lgb * rem_t), 0.0).astype(BF16))
            vs.append(v_h)
            es.append(jnp.exp(lgb * float(CHUNK)))
        upd = _dot(jnp.concatenate(kts, axis=1), jnp.concatenate(vs, axis=0))
        hr_ref[p] = h_pair * jnp.where(row_lo, es[0], es[1]) + upd


def _even_scan(xbc, dtp, proj, cos_t, sin_t, dt_bias, a_log, ret_decay, *, n_ctx, rq_blk, rk_blk, rv_blk):
    b, l, _ = xbc.shape
    nc = l // CHUNK
    cmap = functools.partial(_chunk_of, n_ctx=n_ctx, n_all=nc)
    out_spec = pl.BlockSpec((1, 1, CHUNK, SSD_INNER), lambda bi, d, t: (bi, d, cmap(d, t), 0))
    return pl.pallas_call(
        _even_scan_kernel,
        grid=(b, 2, nc),
        in_specs=[
            pl.BlockSpec((1, CHUNK, SSD_INNER), lambda bi, d, t: (bi, cmap(d, t), 0)),
            pl.BlockSpec((1, CHUNK, 2 * SSD_GROUPS * SSD_STATE), lambda bi, d, t: (bi, cmap(d, t), 2)),
            pl.BlockSpec((1, CHUNK, LANES), lambda bi, d, t: (bi, cmap(d, t), d)),
            pl.BlockSpec((1, CHUNK, RET_QK), lambda bi, d, t: (bi, cmap(d, t), rq_blk)),
            pl.BlockSpec((1, CHUNK, RET_QK), lambda bi, d, t: (bi, cmap(d, t), rk_blk)),
            pl.BlockSpec((1, CHUNK, RET_V), lambda bi, d, t: (bi, cmap(d, t), rv_blk)),
            pl.BlockSpec((CHUNK, LANES), lambda bi, d, t: (cmap(d, t), 0)),
            pl.BlockSpec((CHUNK, LANES), lambda bi, d, t: (cmap(d, t), 0)),
            pl.BlockSpec((1, 1, LANES), lambda bi, d, t: (d, 0, 0)),
            pl.BlockSpec((1, 1, LANES), lambda bi, d, t: (d, 0, 0)),
            pl.BlockSpec((1, RET_HEADS, LANES), lambda bi, d, t: (d, 0, 0)),
        ],
        out_specs=[out_spec, out_spec],
        out_shape=[jax.ShapeDtypeStruct((b, 2, l, SSD_INNER), BF16),
                   jax.ShapeDtypeStruct((b, 2, l, RET_V), BF16)],
        scratch_shapes=[pltpu.VMEM((SSD_HEADS // 2, SSD_STATE, LANES), F32),
                        pltpu.VMEM((RET_HEADS // 2, LANES, RET_V_DIM), F32)],
        compiler_params=_cparams(("parallel", "parallel", "arbitrary")),
        name="even_scan",
    )(xbc, xbc, dtp, proj, proj, proj, cos_t, sin_t, dt_bias, a_log, ret_decay)


def _even_out_kernel(ys0_ref, ys1_ref, yr0_ref, yr1_ref, xs_ref, z_ref, rg_ref, dsk_ref, nw_ref, o_ref):
    def f32(ref, *idx):
        return ref[idx].astype(F32)

    y = (f32(ys0_ref, 0, 0) + f32(ys1_ref, 0, 0) + dsk_ref[...] * f32(xs_ref, 0)) * _silu(f32(z_ref, 0))
    gsz = SSD_INNER // SSD_GROUPS
    for g in range(SSD_GROUPS):
        sl = slice(g * gsz, (g + 1) * gsz)
        o_ref[0, :, sl] = _rms(y[:, sl], nw_ref[:, sl]).astype(BF16)
    yr = f32(yr0_ref, 0, 0) + f32(yr1_ref, 0, 0)
    gate = _silu(f32(rg_ref, 0))
    for hh in range(RET_HEADS):
        sl = slice(hh * RET_V_DIM, (hh + 1) * RET_V_DIM)
        v = yr[:, sl]
        mu = jnp.mean(v, axis=-1, keepdims=True)
        cen = v - mu
        var = jnp.mean(cen * cen, axis=-1, keepdims=True)
        o_ref[0, :, SSD_INNER + hh * RET_V_DIM:SSD_INNER + (hh + 1) * RET_V_DIM] = (
            cen * lax.rsqrt(var + EPS) * gate[:, sl]).astype(BF16)


EVEN_OUT_TILE = ROW_TILE


def _even_out(y_ssd, y_ret, xbc, proj, d_skip, ssd_norm, *, z_blk, rg_blk):
    b, _, l, _ = y_ssd.shape
    tr = EVEN_OUT_TILE
    w = SSD_INNER
    dsk = jnp.repeat(d_skip, SSD_HEAD_DIM).reshape(1, w)

    def dir_spec(dd):
        return pl.BlockSpec((1, 1, tr, w), lambda bi, j: (bi, dd, j, 0))

    def col_spec(blk):
        return pl.BlockSpec((1, tr, w), lambda bi, j: (bi, j, blk))

    vec = pl.BlockSpec((1, w), lambda bi, j: (0, 0))
    return pl.pallas_call(
        _even_out_kernel,
        grid=(b, l // tr),
        in_specs=[dir_spec(0), dir_spec(1), dir_spec(0), dir_spec(1), col_spec(0), col_spec(z_blk),
                  col_spec(rg_blk), vec, vec],
        out_specs=pl.BlockSpec((1, tr, 2 * w), lambda bi, j: (bi, j, 0)),
        out_shape=jax.ShapeDtypeStruct((b, l, 2 * w), BF16),
        compiler_params=_cparams(("parallel", "parallel")),
        name="even_out",
    )(y_ssd, y_ssd, y_ret, y_ret, xbc, proj, proj, dsk, ssd_norm.reshape(1, w))


GDN_LEVELS = 7
GDN_GROUP = 16


def _bdot(a, b):
    return jnp.einsum("gmk,gkn->gmn", a, b, preferred_element_type=F32)


def _bdot_nt(a, b):
    return jnp.einsum("gmk,gnk->gmn", a, b, preferred_element_type=F32)


def _gdn_scan_kernel(q_ref, k_ref, v_ref, raw_ref, bias_ref, alog_ref, lvl_ref, o_ref, s_ref):
    d = pl.program_id(1)
    t = pl.program_id(2)

    @pl.when(t == 0)
    def _():
        s_ref[...] = jnp.zeros_like(s_ref)

    ii, jj, fwd, incl, strict = _scan_masks(d)
    strict_f = jnp.where(strict, 1.0, 0.0)
    eye = jnp.where(ii == jj, 1.0, 0.0)
    tri = jnp.where(incl, 1.0, 0.0).astype(BF16)

    raw = raw_ref[0]
    beta = jax.nn.sigmoid(raw)
    gl = -jnp.exp(alog_ref[0]) * _softplus(raw + bias_ref[0])
    cs = _dot_sel_left(tri, gl)
    tot = jnp.sum(gl, axis=0, keepdims=True)
    cs_t = cs.T
    wend_t = jnp.exp(tot - cs).T

    gsz = GDN_GROUP
    rep = GDN_V_HEADS // GDN_K_HEADS
    kper = gsz // rep
    hd = GDN_HEAD_DIM
    strict_f3 = strict_f[None]
    incl3 = incl[None]

    def col_bcast(a, c):
        return jnp.broadcast_to(a[:, c:c + 1], (CHUNK, LANES))

    def per_key_head(a4, x):
        res = a4[:, None] * x.reshape((kper, rep) + x.shape[1:])
        return res.reshape((gsz,) + res.shape[2:])

    for gi in range(GDN_V_HEADS // gsz):
        heads = range(gi * gsz, (gi + 1) * gsz)
        kheads = range(gi * kper, (gi + 1) * kper)
        q4b = jnp.stack([q_ref[0, :, kh * hd:(kh + 1) * hd] for kh in kheads])
        k4b = jnp.stack([k_ref[0, :, kh * hd:(kh + 1) * hd] for kh in kheads])
        k4 = k4b.astype(F32)
        kk4 = _bdot_nt(k4b, k4b)
        qk4 = _bdot_nt(q4b, k4b)
        kt4 = jnp.stack([k4[i].T for i in range(kper)])
        v = jnp.stack([v_ref[0, :, h * hd:(h + 1) * hd] for h in heads])
        cs_b = jnp.stack([col_bcast(cs, GDN_V_HEADS + h) for h in heads])
        beta_b = jnp.stack([col_bcast(beta, h) for h in heads])
        cs_r = jnp.stack([cs_t[GDN_V_HEADS + h:GDN_V_HEADS + h + 1, :] for h in heads])
        wend_r = jnp.stack([wend_t[GDN_V_HEADS + h:GDN_V_HEADS + h + 1, :] for h in heads])
        gam = jnp.exp(jnp.where(incl3, cs_b - cs_r, NEG_BIG))
        ecs_b = jnp.exp(cs_b)
        etot = jnp.exp(jnp.where(fwd, cs_b[:, CHUNK - 1:CHUNK, :], cs_b[:, 0:1, :]))
        low = per_key_head(kk4, gam * (strict_f3 * beta_b))
        tinv = eye[None] - low * lvl_ref[0][None]
        for lv in range(1, GDN_LEVELS):
            a_off = (low * lvl_ref[lv][None]).astype(BF16)
            x = _bdot(a_off, tinv.astype(BF16))
            tinv = tinv - _bdot(tinv.astype(BF16), x.astype(BF16))
        state = s_ref[gi * gsz:(gi + 1) * gsz]
        ke = per_key_head(k4, ecs_b).astype(BF16)
        qe = per_key_head(q4b.astype(F32), ecs_b).astype(BF16)
        kq_s = _bdot(jnp.concatenate([ke, qe], axis=1), state.astype(BF16))
        resid = (beta_b * (v.astype(F32) - kq_s[:, :CHUNK])).astype(BF16)
        v_new_bf = _bdot(tinv.astype(BF16), resid).astype(BF16)
        out = kq_s[:, CHUNK:] + _bdot(per_key_head(qk4, gam).astype(BF16), v_new_bf)
        for g, h in enumerate(heads):
            o_ref[0, 0, :, h * hd:(h + 1) * hd] = out[g].astype(o_ref.dtype)
        k_end_t = per_key_head(kt4, wend_r).astype(BF16)
        s_ref[gi * gsz:(gi + 1) * gsz] = state * etot + _bdot(k_end_t, v_new_bf)


def _gdn_level_masks():
    i = jnp.arange(CHUNK)[:, None]
    j = jnp.arange(CHUNK)[None, :]
    masks = []
    for lv in range(GDN_LEVELS):
        blk = 1 << lv
        masks.append(((i // (2 * blk) == j // (2 * blk)) & (i // blk != j // blk)).astype(F32))
    return jnp.stack(masks)


def _gdn_scan(qk, v, raw, bias, a_log, *, n_ctx):
    b, l, _ = qk.shape
    nc = l // CHUNK
    cmap = functools.partial(_chunk_of, n_ctx=n_ctx, n_all=nc)
    return pl.pallas_call(
        _gdn_scan_kernel,
        grid=(b, 2, nc),
        in_specs=[
            pl.BlockSpec((1, CHUNK, GDN_K), lambda bi, d, t: (bi, cmap(d, t), 0)),
            pl.BlockSpec((1, CHUNK, GDN_K), lambda bi, d, t: (bi, cmap(d, t), 1)),
            pl.BlockSpec((1, CHUNK, GDN_V), lambda bi, d, t: (bi, cmap(d, t), 0)),
            pl.BlockSpec((1, CHUNK, LANES), lambda bi, d, t: (bi, cmap(d, t), d)),
            pl.BlockSpec((1, 1, LANES), lambda bi, d, t: (d, 0, 0)),
            pl.BlockSpec((1, 1, LANES), lambda bi, d, t: (d, 0, 0)),
            pl.BlockSpec((GDN_LEVELS, CHUNK, CHUNK), lambda bi, d, t: (0, 0, 0)),
        ],
        out_specs=pl.BlockSpec((1, 1, CHUNK, GDN_V), lambda bi, d, t: (bi, d, cmap(d, t), 0)),
        out_shape=jax.ShapeDtypeStruct((b, 2, l, GDN_V), BF16),
        scratch_shapes=[pltpu.VMEM((GDN_V_HEADS, GDN_HEAD_DIM, GDN_HEAD_DIM), F32)],
        compiler_params=_cparams(("parallel", "parallel", "arbitrary")),
        name="gdn_scan",
    )(qk, qk, v, raw, bias, a_log, _gdn_level_masks())


def _odd_out_kernel(o0_ref, o1_ref, z_ref, nw_ref, out_ref):
    o = o0_ref[0, 0].astype(F32) + o1_ref[0, 0].astype(F32)
    gate = _silu(z_ref[0].astype(F32))
    for hh in range(GDN_V_HEADS):
        sl = slice(hh * GDN_HEAD_DIM, (hh + 1) * GDN_HEAD_DIM)
        out_ref[0, :, sl] = (_rms(o[:, sl], nw_ref[...]) * gate[:, sl]).astype(BF16)


def _odd_out(o, proj, norm_w, *, z_blk, row_off):
    b, _, l, w = o.shape
    tr = ROW_TILE
    n_tiles = l // tr - row_off

    def dir_spec(dd):
        return pl.BlockSpec((1, 1, tr, w), lambda bi, j: (bi, dd, j + row_off, 0))

    return pl.pallas_call(
        _odd_out_kernel,
        grid=(b, n_tiles),
        in_specs=[dir_spec(0), dir_spec(1),
                  pl.BlockSpec((1, tr, w), lambda bi, j: (bi, j + row_off, z_blk)),
                  pl.BlockSpec((1, GDN_HEAD_DIM), lambda bi, j: (0, 0))],
        out_specs=pl.BlockSpec((1, tr, w), lambda bi, j: (bi, j, 0)),
        out_shape=jax.ShapeDtypeStruct((b, n_tiles * tr, w), BF16),
        compiler_params=_cparams(("parallel", "parallel")),
        name="odd_out",
    )(o, o, proj, norm_w.reshape(1, GDN_HEAD_DIM))


def _dir_lanes(p, lane_off):
    out = jnp.zeros((2, 1, LANES), F32)
    return out.at[:, 0, lane_off:lane_off + p.shape[1]].set(p.astype(F32))


def _rotary_tables(n_ctx_rows, n_lat_rows):
    nf = RET_QK_DIM // 4
    pos = jnp.arange(n_lat_rows)
    rowp = (pos // GRID_W).astype(F32)
    colp = (pos % GRID_W).astype(F32)
    freqs = ROPE_BASE ** (-jnp.arange(nf, dtype=F32) / nf)
    ang = jnp.concatenate([rowp[:, None] * freqs, colp[:, None] * freqs], axis=-1)
    cos = jnp.cos(ang)
    sin = jnp.sin(ang)
    cos_h = jnp.concatenate([cos, cos], axis=-1)
    sin_h = jnp.concatenate([-sin, sin], axis=-1)
    reps = LANES // RET_QK_DIM
    cos_t = jnp.concatenate([jnp.ones((n_ctx_rows, LANES), F32), jnp.tile(cos_h, (1, reps))], axis=0)
    sin_t = jnp.concatenate([jnp.zeros((n_ctx_rows, LANES), F32), jnp.tile(sin_h, (1, reps))], axis=0)
    return cos_t, sin_t


EV_Z, EV_RG, EV_RV, EV_XBC, EV_RQ, EV_RK = 0, 2048, 4096, 6144, 9216, 10240
EV_MAIN = 11264
SSD_CONV_DIM = SSD_INNER + 2 * SSD_GROUPS * SSD_STATE


def _even_weights(w_in):
    o = 0
    parts = {}
    for name, sz in (("z", SSD_INNER), ("xbc", SSD_CONV_DIM), ("dt", 2 * SSD_HEADS), ("rq", RET_QK),
                     ("rk", RET_QK), ("rv", RET_V), ("rg", RET_V)):
        parts[name] = w_in[:, o:o + sz]
        o += sz
    main = jnp.concatenate([parts[nm].astype(BF16) for nm in ("z", "rg", "rv", "xbc", "rq", "rk")], axis=1)
    pad = jnp.zeros((w_in.shape[0], LANES - SSD_HEADS), w_in.dtype)
    dt = parts["dt"]
    w_dt = jnp.concatenate([dt[:, :SSD_HEADS], pad, dt[:, SSD_HEADS:], pad], axis=1)
    return main[None], w_dt[None]


def _odd_small_weights(w_in):
    base = 2 * GDN_K + 2 * GDN_V
    hv = GDN_V_HEADS
    beta = w_in[:, base:base + 2 * hv]
    dec = w_in[:, base + 2 * hv:base + 4 * hv]
    pad = jnp.zeros((w_in.shape[0], LANES - 2 * hv), w_in.dtype)
    return jnp.concatenate([beta[:, :hv], dec[:, :hv], pad, beta[:, hv:], dec[:, hv:], pad], axis=1)[None]


def _ffn(s, o, mod, layer, post_mix, pre_ffn, w1, w3, w2, *, s_off=0):
    b, l, d = o.shape
    s1, h = _res_norm(s, mod, o=o, gate=(layer, 2), post_w=post_mix, pre_w=pre_ffn,
                      shift=(layer, 3), scale=(layer, 4), s_off=s_off, n_tiles=l // ROW_TILE,
                      has_ctx=s_off == 0)
    d_ff = w1.shape[2]
    act = _matmul(h.reshape(b * l, d), [w1, w3], layer=layer, n=d_ff, tn=256, out_dtype=BF16)
    f = _matmul_ktiled(act, w2, layer=layer)
    return s1, f.reshape(b, l, d)


def kernel(x, c, ctx, c_ctx, ada_w, ada_b, norm_mix_pre, norm_mix_post, norm_ffn_pre, norm_ffn_post,
           ev_w_in, ev_conv_w, ev_conv_b, ev_dt_bias, ev_a_log, ev_d_skip, ev_ssd_norm, ev_ret_decay,
           ev_w_out, od_w_in, od_conv_w, od_dt_bias, od_a_log, od_norm, od_w_out,
           ffn_w1, ffn_w3, ffn_w2):
    b, seq, d = x.shape
    n_ctx_rows = ctx.shape[1]
    assert n_ctx_rows == ROW_TILE and seq % ROW_TILE == 0
    l = n_ctx_rows + seq
    m = b * l
    n_ctx = n_ctx_rows // CHUNK

    mod = _ada(jnp.concatenate([c, c_ctx[None]], axis=0), ada_w, ada_b)

    s, h = _res_norm((ctx, x), mod, pre_w=norm_mix_pre[0], shift=(0, 0), scale=(0, 1))
    w_main, w_dt = _even_weights(ev_w_in[0])
    h2 = h.reshape(m, d)
    proj = _matmul(h2, [w_main], n=EV_MAIN, tn=1024, out_dtype=BF16).reshape(b, l, EV_MAIN)
    dtp = _matmul(h2, [w_dt], n=2 * LANES, tn=2 * LANES, out_dtype=F32).reshape(b, l, 2 * LANES)
    xbc = _conv_silu(proj, ev_conv_w[0], ev_conv_b[0], col_off=EV_XBC, width=SSD_CONV_DIM)
    cos_t, sin_t = _rotary_tables(n_ctx_rows, seq)
    rdec = jnp.broadcast_to(ev_ret_decay[0][:, :, None], (2, RET_HEADS, LANES)).astype(F32)
    y_ssd, y_ret = _even_scan(xbc, dtp, proj, cos_t, sin_t, _dir_lanes(ev_dt_bias[0], 0),
                              _dir_lanes(ev_a_log[0], 0), rdec, n_ctx=n_ctx,
                              rq_blk=EV_RQ // RET_QK, rk_blk=EV_RK // RET_QK, rv_blk=EV_RV // RET_V)
    act = _even_out(y_ssd, y_ret, xbc, proj, ev_d_skip[0], ev_ssd_norm[0],
                    z_blk=EV_Z // SSD_INNER, rg_blk=EV_RG // SSD_INNER)
    o = _matmul(act.reshape(m, act.shape[-1]), [ev_w_out], layer=0, n=d, tn=512, out_dtype=BF16).reshape(b, l, d)
    s, f = _ffn(s, o, mod, 0, norm_mix_post[0], norm_ffn_pre[0], ffn_w1, ffn_w3, ffn_w2)

    s, h = _res_norm(s, mod, o=f, gate=(0, 5), post_w=norm_ffn_post[0], pre_w=norm_mix_pre[1],
                     shift=(1, 0), scale=(1, 1))
    h2 = h.reshape(m, d)
    n_main = 2 * GDN_K + 2 * GDN_V
    proj = _matmul(h2, [od_w_in], layer=0, n=n_main, tn=512, out_dtype=BF16).reshape(b, l, n_main)
    raw = _matmul(h2, [_odd_small_weights(od_w_in[0])], n=2 * LANES, tn=2 * LANES,
                  out_dtype=F32).reshape(b, l, 2 * LANES)
    qk = _conv_silu(proj, od_conv_w[0][:, :2 * GDN_K], None, col_off=0, width=2 * GDN_K,
                    l2_scaled_blocks=GDN_K // CONV_CW, l2_scale=GDN_HEAD_DIM ** -0.5)
    v = _conv_silu(proj, od_conv_w[0][:, 2 * GDN_K:], None, col_off=2 * GDN_K, width=GDN_V)
    o2 = _gdn_scan(qk, v, raw, _dir_lanes(od_dt_bias[0], GDN_V_HEADS), _dir_lanes(od_a_log[0], GDN_V_HEADS),
                   n_ctx=n_ctx)
    ctx_tiles = n_ctx_rows // ROW_TILE
    act = _odd_out(o2, proj, od_norm[0], z_blk=(2 * GDN_K + GDN_V) // GDN_V, row_off=ctx_tiles)
    o = _matmul(act.reshape(b * seq, act.shape[-1]), [od_w_out], layer=0, n=d, tn=512,
                out_dtype=BF16).reshape(b, seq, d)
    s, f = _ffn(s, o, mod, 1, norm_mix_post[1], norm_ffn_pre[1], ffn_w1, ffn_w3, ffn_w2, s_off=ctx_tiles)
    out, _ = _res_norm(s, mod, o=f, gate=(1, 5), post_w=norm_ffn_post[1], has_ctx=False)
    return out
```

```python
import functools
import math

import jax
import jax.numpy as jnp
from jax import lax
from jax.experimental import pallas as pl
from jax.experimental.pallas import tpu as pltpu

F32 = jnp.float32
BF16 = jnp.bfloat16

EPS = 1e-6
CHUNK = 128
LANES = 128
CONV_W = 5
CONV_HALO = 16
GRID_W = 64
ROPE_BASE = 10000.0
ROW_TILE = 256
NEG_BIG = -1e30

SSD_HEADS = 32
SSD_HEAD_DIM = 64
SSD_GROUPS = 4
SSD_STATE = 128
SSD_INNER = SSD_HEADS * SSD_HEAD_DIM
RET_HEADS = 16
RET_QK_DIM = 64
RET_V_DIM = 128
RET_QK = RET_HEADS * RET_QK_DIM
RET_V = RET_HEADS * RET_V_DIM
GDN_K_HEADS = 16
GDN_V_HEADS = 32
GDN_HEAD_DIM = 128
GDN_K = GDN_K_HEADS * GDN_HEAD_DIM
GDN_V = GDN_V_HEADS * GDN_HEAD_DIM

VMEM_LIMIT = 56 * 1024 * 1024


def _cparams(sem):
    return pltpu.CompilerParams(dimension_semantics=sem, vmem_limit_bytes=VMEM_LIMIT)


def _dot(a, b):
    return jnp.dot(a, b, preferred_element_type=F32)


def _dot_nt(a, b):
    return lax.dot_general(a, b, (((1,), (1,)), ((), ())), preferred_element_type=F32)


def _split3(x):
    hi = x.astype(BF16)
    r1 = x - hi.astype(F32)
    mid = r1.astype(BF16)
    lo = (r1 - mid.astype(F32)).astype(BF16)
    return hi, mid, lo


def _dot_sel_left(sel, x):
    hi, mid, lo = _split3(x)
    return _dot(sel, hi) + _dot(sel, mid) + _dot(sel, lo)


def _dot_sel_right(x, sel):
    hi, mid, lo = _split3(x)
    return _dot(hi, sel) + _dot(mid, sel) + _dot(lo, sel)


def _silu(x):
    return x * jax.nn.sigmoid(x)


def _softplus(x):
    return jnp.maximum(x, 0.0) + jnp.log(1.0 + jnp.exp(-jnp.abs(x)))


def _rms(x, w):
    ms = jnp.mean(x * x, axis=-1, keepdims=True)
    return x * lax.rsqrt(ms + EPS) * w


ADA_TN = 512


def _ada_kernel(cb_ref, w_ref, b_ref, o_ref, act_ref, *, n_rows):
    first = jnp.logical_and(pl.program_id(0) == 0, pl.program_id(1) == 0)

    @pl.when(first)
    def _():
        act_ref[...] = _silu(cb_ref[...])

    d = w_ref.shape[1]
    tn = w_ref.shape[2]
    w3 = w_ref[0].reshape(d // 8, 8, tn)
    o_ref[...] = jnp.zeros_like(o_ref)
    for r in range(n_rows):
        a3 = act_ref[r].reshape(d // 8, 8, LANES)
        for lt in range(tn // LANES):
            sl = slice(lt * LANES, (lt + 1) * LANES)
            part = jnp.sum(w3[:, :, sl] * a3, axis=0)
            o_ref[0, r:r + 1, sl] = jnp.sum(part, axis=0, keepdims=True) + b_ref[0, :, sl]


def _ada(cvec, ada_w, ada_b):
    n_rows, d = cvec.shape
    depth, _, n = ada_w.shape
    cb = jnp.broadcast_to(cvec[:, :, None], (n_rows, d, LANES))
    return pl.pallas_call(
        functools.partial(_ada_kernel, n_rows=n_rows),
        grid=(depth, n // ADA_TN),
        in_specs=[
            pl.BlockSpec((n_rows, d, LANES), lambda i, j: (0, 0, 0)),
            pl.BlockSpec((1, d, ADA_TN), lambda i, j: (i, 0, j)),
            pl.BlockSpec((1, 1, ADA_TN), lambda i, j: (i, 0, j)),
        ],
        out_specs=pl.BlockSpec((1, 8, ADA_TN), lambda i, j: (i, 0, j)),
        out_shape=jax.ShapeDtypeStruct((depth, 8, n), F32),
        scratch_shapes=[pltpu.VMEM((n_rows, d, LANES), F32)],
        compiler_params=_cparams(("arbitrary", "arbitrary")),
        name="ada_mod",
    )(cb, ada_w, ada_b.reshape(depth, 1, n))


def _mod_row(m_ref, ctx_row):
    r = pl.program_id(0)
    if ctx_row is not None:
        r = jnp.where(pl.program_id(1) == 0, ctx_row, r)
    return m_ref[0, pl.ds(r, 1), :]


def _res_norm_kernel(*refs, split_in, has_res, has_h, ctx_row):
    it = iter(refs)
    if split_in:
        ctx_ref, x_ref = next(it), next(it)
        s = jnp.where(pl.program_id(1) == 0, ctx_ref[0], x_ref[0])
    else:
        s = next(it)[0]
    if has_res:
        o_ref, gate_ref, post_ref = next(it), next(it), next(it)
    if has_h:
        pre_ref, shift_ref, scale_ref = next(it), next(it), next(it)
    if has_res:
        s = s + _mod_row(gate_ref, ctx_row) * _rms(o_ref[0].astype(F32), post_ref[...])
    if has_res or split_in:
        next(it)[0] = s
    if has_h:
        h = _rms(s, pre_ref[...]) * (1.0 + _mod_row(scale_ref, ctx_row)) + _mod_row(shift_ref, ctx_row)
        next(it)[0] = h.astype(BF16)


def _res_norm(s, mod, *, o=None, gate=None, post_w=None, pre_w=None, shift=None, scale=None,
              s_off=0, o_off=0, n_tiles=None, has_ctx=True):
    split_in = isinstance(s, tuple)
    b, _, d = (s[1] if split_in else s).shape
    has_res = o is not None
    has_h = pre_w is not None
    if n_tiles is None:
        n_tiles = (s[0].shape[1] + s[1].shape[1]) // ROW_TILE if split_in else s.shape[1] // ROW_TILE - s_off
    ctx_row = b if has_ctx else None

    def outrows(bi, j):
        return (bi, j, 0)

    def mod_spec(sel):
        layer, comp = sel
        return pl.BlockSpec((1, 8, d), lambda bi, j: (layer, 0, comp))

    vec_spec = pl.BlockSpec((1, d), lambda bi, j: (0, 0))
    tile = (1, ROW_TILE, d)
    if split_in:
        assert s[0].shape[1] == ROW_TILE and has_ctx and not has_res
        args = list(s)
        in_specs = [pl.BlockSpec(tile, lambda bi, j: (bi, 0, 0)),
                    pl.BlockSpec(tile, lambda bi, j: (bi, jnp.maximum(j - 1, 0), 0))]
    else:
        args, in_specs = [s], [pl.BlockSpec(tile, lambda bi, j: (bi, j + s_off, 0))]
    out_specs, out_shape = [], []
    if has_res:
        args += [o, mod, post_w.reshape(1, d)]
        in_specs += [pl.BlockSpec(tile, lambda bi, j: (bi, j + o_off, 0)), mod_spec(gate), vec_spec]
    if has_res or split_in:
        out_specs.append(pl.BlockSpec(tile, outrows))
        out_shape.append(jax.ShapeDtypeStruct((b, n_tiles * ROW_TILE, d), F32))
    if has_h:
        args += [pre_w.reshape(1, d), mod, mod]
        in_specs += [vec_spec, mod_spec(shift), mod_spec(scale)]
        out_specs.append(pl.BlockSpec(tile, outrows))
        out_shape.append(jax.ShapeDtypeStruct((b, n_tiles * ROW_TILE, d), BF16))
    outs = pl.pallas_call(
        functools.partial(_res_norm_kernel, split_in=split_in, has_res=has_res, has_h=has_h, ctx_row=ctx_row),
        grid=(b, n_tiles),
        in_specs=in_specs,
        out_specs=out_specs,
        out_shape=out_shape,
        compiler_params=_cparams(("parallel", "parallel")),
        name="res_norm",
    )(*args)
    outs = list(outs)
    s_new = outs.pop(0) if (has_res or split_in) else None
    h = outs.pop(0) if has_h else None
    return s_new, h


MM_TM = 2048


def _row_tile(m, cap=MM_TM):
    best = ROW_TILE
    for tm in range(ROW_TILE, cap + 1, ROW_TILE):
        if m % tm == 0:
            best = tm
    return best


def _mm_kernel(a_ref, *refs, swiglu):
    a = a_ref[...]
    if swiglu:
        w1_ref, w3_ref, o_ref = refs
        g = _dot(a, w1_ref[0].astype(BF16))
        u = _dot(a, w3_ref[0].astype(BF16))
        o_ref[...] = (_silu(g) * u).astype(o_ref.dtype)
    else:
        w_ref, o_ref = refs
        o_ref[...] = _dot(a, w_ref[0].astype(BF16)).astype(o_ref.dtype)


def _matmul(a, ws, *, layer=0, col_off=0, n, tn, out_dtype):
    m, k = a.shape
    swiglu = len(ws) == 2
    tm = _row_tile(m)
    assert col_off % tn == 0
    off = col_off // tn
    w_spec = pl.BlockSpec((1, k, tn), lambda i, j: (layer, 0, j + off))
    return pl.pallas_call(
        functools.partial(_mm_kernel, swiglu=swiglu),
        grid=(m // tm, pl.cdiv(n, tn)),
        in_specs=[pl.BlockSpec((tm, k), lambda i, j: (i, 0), pipeline_mode=pl.Buffered(1))]
        + [w_spec] * len(ws),
        out_specs=pl.BlockSpec((tm, tn), lambda i, j: (i, j)),
        out_shape=jax.ShapeDtypeStruct((m, n), out_dtype),
        compiler_params=_cparams(("parallel", "arbitrary")),
        name="matmul_swiglu" if swiglu else "matmul",
    )(a, *ws)


MMK_TN = 1024
MMK_TK = 1024


def _mm_ktiled_kernel(a_ref, w_ref, o_ref, acc_ref, *, k_total, tk):
    kk = pl.program_id(2)
    nk = pl.num_programs(2)

    @pl.when(kk == 0)
    def _():
        acc_ref[...] = jnp.zeros_like(acc_ref)

    rem = k_total - (pl.cdiv(k_total, tk) - 1) * tk

    def step(a, w):
        acc_ref[...] += _dot(a, w.astype(BF16))

    if rem == tk:
        step(a_ref[...], w_ref[0])
    else:
        @pl.when(kk < nk - 1)
        def _():
            step(a_ref[...], w_ref[0])

        @pl.when(kk == nk - 1)
        def _():
            a = a_ref[...]
            w = w_ref[0]
            ka = lax.broadcasted_iota(jnp.int32, a.shape, 1)
            kw = lax.broadcasted_iota(jnp.int32, w.shape, 0)
            step(jnp.where(ka < rem, a, jnp.zeros_like(a)), jnp.where(kw < rem, w, jnp.zeros_like(w)))

    @pl.when(kk == nk - 1)
    def _():
        o_ref[...] = acc_ref[...].astype(o_ref.dtype)


def _matmul_ktiled(a, w, *, layer, tk=MMK_TK):
    m, k = a.shape
    n = w.shape[2]
    tm = _row_tile(m)
    tn = min(MMK_TN, n)
    assert n % tn == 0
    return pl.pallas_call(
        functools.partial(_mm_ktiled_kernel, k_total=k, tk=tk),
        grid=(m // tm, n // tn, pl.cdiv(k, tk)),
        in_specs=[pl.BlockSpec((tm, tk), lambda i, j, kk: (i, kk)),
                  pl.BlockSpec((1, tk, tn), lambda i, j, kk: (layer, kk, j))],
        out_specs=pl.BlockSpec((tm, tn), lambda i, j, kk: (i, j)),
        out_shape=jax.ShapeDtypeStruct((m, n), BF16),
        scratch_shapes=[pltpu.VMEM((tm, tn), F32)],
        compiler_params=_cparams(("parallel", "parallel", "arbitrary")),
        name="matmul_ktiled",
    )(a, w)


CONV_CW = 1024


def _conv_kernel(cur_ref, prev_ref, next_ref, w_ref, *refs, has_bias, l2_scaled_blocks, l2_scale):
    if has_bias:
        b_ref, o_ref = refs
    else:
        (o_ref,) = refs
    j = pl.program_id(1)
    nj = pl.num_programs(1)
    tr = cur_ref.shape[1]
    use_prev = jnp.logical_and(j != 0, j != 1)
    use_next = jnp.logical_and(j != 0, j != nj - 1)
    zeros = jnp.zeros(prev_ref.shape[1:], F32)
    xext = jnp.concatenate([jnp.where(use_prev, prev_ref[0].astype(F32), zeros),
                            cur_ref[0].astype(F32),
                            jnp.where(use_next, next_ref[0].astype(F32), zeros)], axis=0)
    n_ext = xext.shape[0]
    acc = None
    for t in range(CONV_W):
        shift = (CONV_W // 2 - t) % n_ext
        tap = pltpu.roll(xext, shift, axis=0) if shift else xext
        term = w_ref[t:t + 1, :] * tap[CONV_HALO:CONV_HALO + tr, :]
        acc = term if acc is None else acc + term
    if has_bias:
        acc = acc + b_ref[...]
    y = _silu(acc)
    if l2_scaled_blocks is None:
        o_ref[0] = y.astype(o_ref.dtype)
    else:
        scale = jnp.where(pl.program_id(2) < l2_scaled_blocks, l2_scale, 1.0).astype(F32)
        for hd in range(y.shape[1] // LANES):
            sl = slice(hd * LANES, (hd + 1) * LANES)
            yh = y[:, sl]
            ss = jnp.sum(yh * yh, axis=-1, keepdims=True)
            o_ref[0, :, sl] = (yh * (lax.rsqrt(ss + EPS) * scale)).astype(o_ref.dtype)


def _conv_silu(proj, conv_w, conv_b, *, col_off, width, l2_scaled_blocks=None, l2_scale=1.0):
    b, l, _ = proj.shape
    assert col_off % CONV_CW == 0 and width % CONV_CW == 0 and l % ROW_TILE == 0
    coff = col_off // CONV_CW
    hpt = ROW_TILE // CONV_HALO
    n_halo = l // CONV_HALO
    has_bias = conv_b is not None
    args = [proj, proj, proj, conv_w]
    in_specs = [
        pl.BlockSpec((1, ROW_TILE, CONV_CW), lambda bi, j, c: (bi, j, c + coff)),
        pl.BlockSpec((1, CONV_HALO, CONV_CW), lambda bi, j, c: (bi, jnp.maximum(j * hpt - 1, 0), c + coff)),
        pl.BlockSpec((1, CONV_HALO, CONV_CW),
                     lambda bi, j, c: (bi, jnp.minimum((j + 1) * hpt, n_halo - 1), c + coff)),
        pl.BlockSpec((CONV_W, CONV_CW), lambda bi, j, c: (0, c)),
    ]
    if has_bias:
        args.append(conv_b.reshape(1, width))
        in_specs.append(pl.BlockSpec((1, CONV_CW), lambda bi, j, c: (0, c)))
    return pl.pallas_call(
        functools.partial(_conv_kernel, has_bias=has_bias, l2_scaled_blocks=l2_scaled_blocks, l2_scale=l2_scale),
        grid=(b, l // ROW_TILE, width // CONV_CW),
        in_specs=in_specs,
        out_specs=pl.BlockSpec((1, ROW_TILE, CONV_CW), lambda bi, j, c: (bi, j, c)),
        out_shape=jax.ShapeDtypeStruct((b, l, width), BF16),
        compiler_params=_cparams(("parallel", "parallel", "parallel")),
        name="conv_silu",
    )(*args)


def _chunk_of(d, t, n_ctx, n_all):
    rev = jnp.where(t < n_ctx, n_ctx - 1 - t, n_all - 1 - (t - n_ctx))
    return jnp.where(d == 0, t, rev)


def _scan_masks(d, reps=1):
    ii = lax.broadcasted_iota(jnp.int32, (CHUNK, reps * CHUNK), 0)
    jj = lax.broadcasted_iota(jnp.int32, (CHUNK, reps * CHUNK), 1) % CHUNK
    fwd = d == 0
    later = jnp.where(fwd, ii, jj)
    earlier = jnp.where(fwd, jj, ii)
    incl = earlier <= later
    strict = earlier < later
    return ii, jj, fwd, incl, strict


def _even_scan_kernel(xs_ref, bc_ref, dt_ref, rq_ref, rk_ref, rv_ref, cos_ref, sin_ref,
                      dtb_ref, alog_ref, rdec_ref, yssd_ref, yret_ref, hs_ref, hr_ref):
    d = pl.program_id(1)
    t = pl.program_id(2)

    @pl.when(t == 0)
    def _():
        hs_ref[...] = jnp.zeros_like(hs_ref)
        hr_ref[...] = jnp.zeros_like(hr_ref)

    ii, jj, fwd, incl, _ = _scan_masks(d)
    lane = lax.broadcasted_iota(jnp.int32, (CHUNK, LANES), 1)
    row = lax.broadcasted_iota(jnp.int32, (CHUNK, LANES), 0)
    lane_lo = lane < (LANES // 2)
    row_lo = row < (CHUNK // 2)

    dtf = _softplus(dt_ref[0] + dtb_ref[0])
    la = -jnp.exp(alog_ref[0]) * dtf
    tri = jnp.where(incl, 1.0, 0.0).astype(BF16)
    cs = _dot_sel_left(tri, la)
    tot = jnp.sum(la, axis=0, keepdims=True)
    cs_t = cs.T
    dt_t = dtf.T
    wend_t = (dtf * jnp.exp(tot - cs)).T

    bc = bc_ref[0].astype(F32)
    n_state = SSD_GROUPS * SSD_STATE
    for g in range(SSD_GROUPS):
        bm = bc[:, g * SSD_STATE:(g + 1) * SSD_STATE]
        cm = bc[:, n_state + g * SSD_STATE:n_state + (g + 1) * SSD_STATE]
        bm_bf = bm.astype(BF16)
        cm_bf = cm.astype(BF16)
        scores = _dot_nt(cm_bf, bm_bf)
        bm_t = bm.T
        heads_per_group = SSD_HEADS // SSD_GROUPS
        for pp in range(heads_per_group // 2):
            pidx = g * (heads_per_group // 2) + pp
            xs_pair = xs_ref[0, :, pidx * LANES:(pidx + 1) * LANES]
            x_lo = jnp.where(lane_lo, xs_pair, 0.0).astype(BF16)
            x_hi = jnp.where(lane_lo, 0.0, xs_pair).astype(BF16)
            h_pair = hs_ref[pidx]
            h_lo = jnp.where(lane_lo, h_pair, 0.0).astype(BF16)
            h_hi = jnp.where(lane_lo, 0.0, h_pair).astype(BF16)
            ms, qs, kws, ets = [], [], [], []
            for r in (2 * pidx, 2 * pidx + 1):
                cs_b = jnp.broadcast_to(cs[:, r:r + 1], (CHUNK, LANES))
                dec = jnp.exp(jnp.where(incl, cs_b - cs_t[r:r + 1, :], NEG_BIG))
                ms.append((scores * dec * dt_t[r:r + 1, :]).astype(BF16))
                qs.append((cm * jnp.exp(cs_b)).astype(BF16))
                kws.append((bm_t * wend_t[r:r + 1, :]).astype(BF16))
                ets.append(jnp.exp(jnp.where(fwd, cs_b[CHUNK - 1:CHUNK, :], cs_b[0:1, :])))
            lhs = jnp.concatenate(ms + qs, axis=1)
            rhs = jnp.concatenate([x_lo, x_hi, h_lo, h_hi], axis=0)
            yssd_ref[0, 0, :, pidx * LANES:(pidx + 1) * LANES] = _dot(lhs, rhs).astype(yssd_ref.dtype)
            et = jnp.where(lane_lo, jnp.broadcast_to(ets[0], (CHUNK, LANES)),
                           jnp.broadcast_to(ets[1], (CHUNK, LANES)))
            upd = _dot(jnp.concatenate(kws, axis=1), jnp.concatenate([x_lo, x_hi], axis=0))
            hs_ref[pidx] = h_pair * et + upd

    lg = jax.nn.log_sigmoid(rdec_ref[0])
    dist = jnp.where(fwd, ii - jj, jj - ii).astype(F32)
    cnt = jnp.where(fwd, ii + 1, CHUNK - ii).astype(F32)
    rem_t = jnp.where(fwd, CHUNK - 1 - jj, jj).astype(F32)
    cos = cos_ref[...]
    sin = sin_ref[...]
    half = RET_QK_DIM // 2
    swap_lo = (lane % RET_QK_DIM) < half

    def rotary(tt):
        swapped = jnp.where(swap_lo, pltpu.roll(tt, LANES - half, axis=1), pltpu.roll(tt, half, axis=1))
        return tt * cos + swapped * sin

    for p in range(RET_HEADS // 2):
        sl = slice(p * LANES, (p + 1) * LANES)
        q_pair = rotary(rq_ref[0, :, sl].astype(F32))
        k_pair = rotary(rk_ref[0, :, sl].astype(F32) * (RET_QK_DIM ** -0.5))
        k_pair_bf = k_pair.astype(BF16)
        k_t = k_pair.T
        h_pair = hr_ref[p]
        h_bf = h_pair.astype(BF16)
        kts, vs, es = [], [], []
        for side, hh in enumerate((2 * p, 2 * p + 1)):
            lgb = jnp.broadcast_to(lg[hh:hh + 1, :], (CHUNK, LANES))
            keep = lane_lo if side == 0 else jnp.logical_not(lane_lo)
            keep_rows = row_lo if side == 0 else jnp.logical_not(row_lo)
            q_h = jnp.where(keep, q_pair, 0.0)
            scores = _dot_nt(q_h.astype(BF16), k_pair_bf)
            dec = jnp.where(incl, jnp.exp(lgb * dist), 0.0)
            m = (scores * dec).astype(BF16)
            qe = (q_h * jnp.exp(lgb * cnt)).astype(BF16)
            v_h = rv_ref[0, :, hh * RET_V_DIM:(hh + 1) * RET_V_DIM].astype(BF16)
            y = _dot(jnp.concatenate([m, qe], axis=1), jnp.concatenate([v_h, h_bf], axis=0))
            yret_ref[0, 0, :, hh * RET_V_DIM:(hh + 1) * RET_V_DIM] = y.astype(yret_ref.dtype)
            kts.append(jnp.where(keep_rows, k_t * jnp.exp(lgb * rem_t), 0.0).astype(BF16))
            vs.append(v_h)
            es.append(jnp.exp(lgb * float(CHUNK)))
        upd = _dot(jnp.concatenate(kts, axis=1), jnp.concatenate(vs, axis=0))
        hr_ref[p] = h_pair * jnp.where(row_lo, es[0], es[1]) + upd


def _even_scan(xbc, dtp, proj, cos_t, sin_t, dt_bias, a_log, ret_decay, *, n_ctx, rq_blk, rk_blk, rv_blk):
    b, l, _ = xbc.shape
    nc = l // CHUNK
    cmap = functools.partial(_chunk_of, n_ctx=n_ctx, n_all=nc)
    out_spec = pl.BlockSpec((1, 1, CHUNK, SSD_INNER), lambda bi, d, t: (bi, d, cmap(d, t), 0))
    return pl.pallas_call(
        _even_scan_kernel,
        grid=(b, 2, nc),
        in_specs=[
            pl.BlockSpec((1, CHUNK, SSD_INNER), lambda bi, d, t: (bi, cmap(d, t), 0)),
            pl.BlockSpec((1, CHUNK, 2 * SSD_GROUPS * SSD_STATE), lambda bi, d, t: (bi, cmap(d, t), 2)),
            pl.BlockSpec((1, CHUNK, LANES), lambda bi, d, t: (bi, cmap(d, t), d)),
            pl.BlockSpec((1, CHUNK, RET_QK), lambda bi, d, t: (bi, cmap(d, t), rq_blk)),
            pl.BlockSpec((1, CHUNK, RET_QK), lambda bi, d, t: (bi, cmap(d, t), rk_blk)),
            pl.BlockSpec((1, CHUNK, RET_V), lambda bi, d, t: (bi, cmap(d, t), rv_blk)),
            pl.BlockSpec((CHUNK, LANES), lambda bi, d, t: (cmap(d, t), 0)),
            pl.BlockSpec((CHUNK, LANES), lambda bi, d, t: (cmap(d, t), 0)),
            pl.BlockSpec((1, 1, LANES), lambda bi, d, t: (d, 0, 0)),
            pl.BlockSpec((1, 1, LANES), lambda bi, d, t: (d, 0, 0)),
            pl.BlockSpec((1, RET_HEADS, LANES), lambda bi, d, t: (d, 0, 0)),
        ],
        out_specs=[out_spec, out_spec],
        out_shape=[jax.ShapeDtypeStruct((b, 2, l, SSD_INNER), BF16),
                   jax.ShapeDtypeStruct((b, 2, l, RET_V), BF16)],
        scratch_shapes=[pltpu.VMEM((SSD_HEADS // 2, SSD_STATE, LANES), F32),
                        pltpu.VMEM((RET_HEADS // 2, LANES, RET_V_DIM), F32)],
        compiler_params=_cparams(("parallel", "parallel", "arbitrary")),
        name="even_scan",
    )(xbc, xbc, dtp, proj, proj, proj, cos_t, sin_t, dt_bias, a_log, ret_decay)


def _even_out_kernel(ys0_ref, ys1_ref, yr0_ref, yr1_ref, xs_ref, z_ref, rg_ref, dsk_ref, nw_ref, o_ref):
    def f32(ref, *idx):
        return ref[idx].astype(F32)

    y = (f32(ys0_ref, 0, 0) + f32(ys1_ref, 0, 0) + dsk_ref[...] * f32(xs_ref, 0)) * _silu(f32(z_ref, 0))
    gsz = SSD_INNER // SSD_GROUPS
    for g in range(SSD_GROUPS):
        sl = slice(g * gsz, (g + 1) * gsz)
        o_ref[0, :, sl] = _rms(y[:, sl], nw_ref[:, sl]).astype(BF16)
    yr = f32(yr0_ref, 0, 0) + f32(yr1_ref, 0, 0)
    gate = _silu(f32(rg_ref, 0))
    for hh in range(RET_HEADS):
        sl = slice(hh * RET_V_DIM, (hh + 1) * RET_V_DIM)
        v = yr[:, sl]
        mu = jnp.mean(v, axis=-1, keepdims=True)
        cen = v - mu
        var = jnp.mean(cen * cen, axis=-1, keepdims=True)
        o_ref[0, :, SSD_INNER + hh * RET_V_DIM:SSD_INNER + (hh + 1) * RET_V_DIM] = (
            cen * lax.rsqrt(var + EPS) * gate[:, sl]).astype(BF16)


EVEN_OUT_TILE = ROW_TILE


def _even_out(y_ssd, y_ret, xbc, z_src, rg_src, d_skip, ssd_norm, *, z_blk, rg_blk):
    b, _, l, _ = y_ssd.shape
    tr = EVEN_OUT_TILE
    w = SSD_INNER
    dsk = jnp.repeat(d_skip, SSD_HEAD_DIM).reshape(1, w)

    def dir_spec(dd):
        return pl.BlockSpec((1, 1, tr, w), lambda bi, j: (bi, dd, j, 0))

    def col_spec(blk):
        return pl.BlockSpec((1, tr, w), lambda bi, j: (bi, j, blk))

    vec = pl.BlockSpec((1, w), lambda bi, j: (0, 0))
    return pl.pallas_call(
        _even_out_kernel,
        grid=(b, l // tr),
        in_specs=[dir_spec(0), dir_spec(1), dir_spec(0), dir_spec(1), col_spec(0), col_spec(z_blk),
                  col_spec(rg_blk), vec, vec],
        out_specs=pl.BlockSpec((1, tr, 2 * w), lambda bi, j: (bi, j, 0)),
        out_shape=jax.ShapeDtypeStruct((b, l, 2 * w), BF16),
        compiler_params=_cparams(("parallel", "parallel")),
        name="even_out",
    )(y_ssd, y_ssd, y_ret, y_ret, xbc, z_src, rg_src, dsk, ssd_norm.reshape(1, w))


GDN_LEVELS = 7
GDN_GROUP = 16


def _bdot(a, b):
    return jnp.einsum("gmk,gkn->gmn", a, b, preferred_element_type=F32)


def _bdot_nt(a, b):
    return jnp.einsum("gmk,gnk->gmn", a, b, preferred_element_type=F32)


def _gdn_scan_kernel(q_ref, k_ref, v_ref, raw_ref, bias_ref, alog_ref, lvl_ref, o_ref, s_ref):
    d = pl.program_id(1)
    t = pl.program_id(2)

    @pl.when(t == 0)
    def _():
        s_ref[...] = jnp.zeros_like(s_ref)

    ii, jj, fwd, incl, strict = _scan_masks(d)
    strict_f = jnp.where(strict, 1.0, 0.0)
    eye = jnp.where(ii == jj, 1.0, 0.0)
    tri = jnp.where(incl, 1.0, 0.0).astype(BF16)

    raw = raw_ref[0]
    beta = jax.nn.sigmoid(raw)
    gl = -jnp.exp(alog_ref[0]) * _softplus(raw + bias_ref[0])
    cs = _dot_sel_left(tri, gl)
    tot = jnp.sum(gl, axis=0, keepdims=True)
    cs_t = cs.T
    wend_t = jnp.exp(tot - cs).T

    gsz = GDN_GROUP
    rep = GDN_V_HEADS // GDN_K_HEADS
    kper = gsz // rep
    hd = GDN_HEAD_DIM
    strict_f3 = strict_f[None]
    incl3 = incl[None]

    def col_bcast(a, c):
        return jnp.broadcast_to(a[:, c:c + 1], (CHUNK, LANES))

    def per_key_head(a4, x):
        res = a4[:, None] * x.reshape((kper, rep) + x.shape[1:])
        return res.reshape((gsz,) + res.shape[2:])

    for gi in range(GDN_V_HEADS // gsz):
        heads = range(gi * gsz, (gi + 1) * gsz)
        kheads = range(gi * kper, (gi + 1) * kper)
        q4b = jnp.stack([q_ref[0, :, kh * hd:(kh + 1) * hd] for kh in kheads])
        k4b = jnp.stack([k_ref[0, :, kh * hd:(kh + 1) * hd] for kh in kheads])
        k4 = k4b.astype(F32)
        kk4 = _bdot_nt(k4b, k4b)
        qk4 = _bdot_nt(q4b, k4b)
        kt4 = jnp.stack([k4[i].T for i in range(kper)])
        v = jnp.stack([v_ref[0, :, h * hd:(h + 1) * hd] for h in heads])
        cs_b = jnp.stack([col_bcast(cs, GDN_V_HEADS + h) for h in heads])
        beta_b = jnp.stack([col_bcast(beta, h) for h in heads])
        cs_r = jnp.stack([cs_t[GDN_V_HEADS + h:GDN_V_HEADS + h + 1, :] for h in heads])
        wend_r = jnp.stack([wend_t[GDN_V_HEADS + h:GDN_V_HEADS + h + 1, :] for h in heads])
        gam = jnp.exp(jnp.where(incl3, cs_b - cs_r, NEG_BIG))
        ecs_b = jnp.exp(cs_b)
        etot = jnp.exp(jnp.where(fwd, cs_b[:, CHUNK - 1:CHUNK, :], cs_b[:, 0:1, :]))
        low = per_key_head(kk4, gam * (strict_f3 * beta_b))
        tinv = eye[None] - low * lvl_ref[0][None]
        for lv in range(1, GDN_LEVELS):
            a_off = (low * lvl_ref[lv][None]).astype(BF16)
            x = _bdot(a_off, tinv.astype(BF16))
            tinv = tinv - _bdot(tinv.astype(BF16), x.astype(BF16))
        state = s_ref[gi * gsz:(gi + 1) * gsz]
        ke = per_key_head(k4, ecs_b).astype(BF16)
        qe = per_key_head(q4b.astype(F32), ecs_b).astype(BF16)
        kq_s = _bdot(jnp.concatenate([ke, qe], axis=1), state.astype(BF16))
        resid = (beta_b * (v.astype(F32) - kq_s[:, :CHUNK])).astype(BF16)
        v_new_bf = _bdot(tinv.astype(BF16), resid).astype(BF16)
        out = kq_s[:, CHUNK:] + _bdot(per_key_head(qk4, gam).astype(BF16), v_new_bf)
        for g, h in enumerate(heads):
            o_ref[0, 0, :, h * hd:(h + 1) * hd] = out[g].astype(o_ref.dtype)
        k_end_t = per_key_head(kt4, wend_r).astype(BF16)
        s_ref[gi * gsz:(gi + 1) * gsz] = state * etot + _bdot(k_end_t, v_new_bf)


def _gdn_level_masks():
    i = jnp.arange(CHUNK)[:, None]
    j = jnp.arange(CHUNK)[None, :]
    masks = []
    for lv in range(GDN_LEVELS):
        blk = 1 << lv
        masks.append(((i // (2 * blk) == j // (2 * blk)) & (i // blk != j // blk)).astype(F32))
    return jnp.stack(masks)


def _gdn_scan(qk, v, raw, bias, a_log, *, n_ctx):
    b, l, _ = qk.shape
    nc = l // CHUNK
    cmap = functools.partial(_chunk_of, n_ctx=n_ctx, n_all=nc)
    return pl.pallas_call(
        _gdn_scan_kernel,
        grid=(b, 2, nc),
        in_specs=[
            pl.BlockSpec((1, CHUNK, GDN_K), lambda bi, d, t: (bi, cmap(d, t), 0)),
            pl.BlockSpec((1, CHUNK, GDN_K), lambda bi, d, t: (bi, cmap(d, t), 1)),
            pl.BlockSpec((1, CHUNK, GDN_V), lambda bi, d, t: (bi, cmap(d, t), 0)),
            pl.BlockSpec((1, CHUNK, LANES), lambda bi, d, t: (bi, cmap(d, t), d)),
            pl.BlockSpec((1, 1, LANES), lambda bi, d, t: (d, 0, 0)),
            pl.BlockSpec((1, 1, LANES), lambda bi, d, t: (d, 0, 0)),
            pl.BlockSpec((GDN_LEVELS, CHUNK, CHUNK), lambda bi, d, t: (0, 0, 0)),
        ],
        out_specs=pl.BlockSpec((1, 1, CHUNK, GDN_V), lambda bi, d, t: (bi, d, cmap(d, t), 0)),
        out_shape=jax.ShapeDtypeStruct((b, 2, l, GDN_V), BF16),
        scratch_shapes=[pltpu.VMEM((GDN_V_HEADS, GDN_HEAD_DIM, GDN_HEAD_DIM), F32)],
        compiler_params=_cparams(("parallel", "parallel", "arbitrary")),
        name="gdn_scan",
    )(qk, qk, v, raw, bias, a_log, _gdn_level_masks())


def _odd_out_kernel(o0_ref, o1_ref, z_ref, nw_ref, out_ref):
    o = o0_ref[0, 0].astype(F32) + o1_ref[0, 0].astype(F32)
    gate = _silu(z_ref[0].astype(F32))
    for hh in range(GDN_V_HEADS):
        sl = slice(hh * GDN_HEAD_DIM, (hh + 1) * GDN_HEAD_DIM)
        out_ref[0, :, sl] = (_rms(o[:, sl], nw_ref[...]) * gate[:, sl]).astype(BF16)


def _odd_out(o, proj, norm_w, *, z_blk, row_off):
    b, _, l, w = o.shape
    tr = ROW_TILE
    n_tiles = l // tr - row_off

    def dir_spec(dd):
        return pl.BlockSpec((1, 1, tr, w), lambda bi, j: (bi, dd, j + row_off, 0))

    return pl.pallas_call(
        _odd_out_kernel,
        grid=(b, n_tiles),
        in_specs=[dir_spec(0), dir_spec(1),
                  pl.BlockSpec((1, tr, w), lambda bi, j: (bi, j + row_off, z_blk)),
                  pl.BlockSpec((1, GDN_HEAD_DIM), lambda bi, j: (0, 0))],
        out_specs=pl.BlockSpec((1, tr, w), lambda bi, j: (bi, j, 0)),
        out_shape=jax.ShapeDtypeStruct((b, n_tiles * tr, w), BF16),
        compiler_params=_cparams(("parallel", "parallel")),
        name="odd_out",
    )(o, o, proj, norm_w.reshape(1, GDN_HEAD_DIM))


def _dir_lanes(p, lane_off):
    out = jnp.zeros((2, 1, LANES), F32)
    return out.at[:, 0, lane_off:lane_off + p.shape[1]].set(p.astype(F32))


def _rotary_tables(n_ctx_rows, n_lat_rows):
    nf = RET_QK_DIM // 4
    pos = jnp.arange(n_lat_rows)
    rowp = (pos // GRID_W).astype(F32)
    colp = (pos % GRID_W).astype(F32)
    freqs = ROPE_BASE ** (-jnp.arange(nf, dtype=F32) / nf)
    ang = jnp.concatenate([rowp[:, None] * freqs, colp[:, None] * freqs], axis=-1)
    cos = jnp.cos(ang)
    sin = jnp.sin(ang)
    cos_h = jnp.concatenate([cos, cos], axis=-1)
    sin_h = jnp.concatenate([-sin, sin], axis=-1)
    reps = LANES // RET_QK_DIM
    cos_t = jnp.concatenate([jnp.ones((n_ctx_rows, LANES), F32), jnp.tile(cos_h, (1, reps))], axis=0)
    sin_t = jnp.concatenate([jnp.zeros((n_ctx_rows, LANES), F32), jnp.tile(sin_h, (1, reps))], axis=0)
    return cos_t, sin_t


SSD_CONV_DIM = SSD_INNER + 2 * SSD_GROUPS * SSD_STATE
EV_HEAD = SSD_INNER + SSD_CONV_DIM
EV_TAIL_OFF = EV_HEAD + 2 * SSD_HEADS
EV_TAIL = 2 * RET_QK + 2 * RET_V
EVH_Z, EVH_XBC = 0, SSD_INNER
EVT_RQ, EVT_RK, EVT_RV, EVT_RG = 0, RET_QK, 2 * RET_QK, 2 * RET_QK + RET_V


def _even_weights(w_in):
    tail = w_in[:, EV_TAIL_OFF:EV_TAIL_OFF + EV_TAIL].astype(BF16)
    pad = jnp.zeros((w_in.shape[0], LANES - SSD_HEADS), w_in.dtype)
    dt = w_in[:, EV_HEAD:EV_TAIL_OFF]
    w_dt = jnp.concatenate([dt[:, :SSD_HEADS], pad, dt[:, SSD_HEADS:], pad], axis=1)
    return tail[None], w_dt[None]


def _odd_small_weights(w_in):
    base = 2 * GDN_K + 2 * GDN_V
    hv = GDN_V_HEADS
    beta = w_in[:, base:base + 2 * hv]
    dec = w_in[:, base + 2 * hv:base + 4 * hv]
    pad = jnp.zeros((w_in.shape[0], LANES - 2 * hv), w_in.dtype)
    return jnp.concatenate([beta[:, :hv], dec[:, :hv], pad, beta[:, hv:], dec[:, hv:], pad], axis=1)[None]


def _ffn(s, o, mod, layer, post_mix, pre_ffn, w1, w3, w2, *, s_off=0):
    b, l, d = o.shape
    s1, h = _res_norm(s, mod, o=o, gate=(layer, 2), post_w=post_mix, pre_w=pre_ffn,
                      shift=(layer, 3), scale=(layer, 4), s_off=s_off, n_tiles=l // ROW_TILE,
                      has_ctx=s_off == 0)
    d_ff = w1.shape[2]
    act = _matmul(h.reshape(b * l, d), [w1, w3], layer=layer, n=d_ff, tn=256, out_dtype=BF16)
    f = _matmul_ktiled(act, w2, layer=layer)
    return s1, f.reshape(b, l, d)


def kernel(x, c, ctx, c_ctx, ada_w, ada_b, norm_mix_pre, norm_mix_post, norm_ffn_pre, norm_ffn_post,
           ev_w_in, ev_conv_w, ev_conv_b, ev_dt_bias, ev_a_log, ev_d_skip, ev_ssd_norm, ev_ret_decay,
           ev_w_out, od_w_in, od_conv_w, od_dt_bias, od_a_log, od_norm, od_w_out,
           ffn_w1, ffn_w3, ffn_w2):
    b, seq, d = x.shape
    n_ctx_rows = ctx.shape[1]
    assert n_ctx_rows == ROW_TILE and seq % ROW_TILE == 0
    l = n_ctx_rows + seq
    m = b * l
    n_ctx = n_ctx_rows // CHUNK

    mod = _ada(jnp.concatenate([c, c_ctx[None]], axis=0), ada_w, ada_b)

    s, h = _res_norm((ctx, x), mod, pre_w=norm_mix_pre[0], shift=(0, 0), scale=(0, 1))
    w_tail, w_dt = _even_weights(ev_w_in[0])
    h2 = h.reshape(m, d)
    proj_h = _matmul(h2, [ev_w_in], layer=0, n=EV_HEAD, tn=512, out_dtype=BF16).reshape(b, l, EV_HEAD)
    proj_t = _matmul(h2, [w_tail], n=EV_TAIL, tn=1024, out_dtype=BF16).reshape(b, l, EV_TAIL)
    dtp = _matmul(h2, [w_dt], n=2 * LANES, tn=2 * LANES, out_dtype=F32).reshape(b, l, 2 * LANES)
    xbc = _conv_silu(proj_h, ev_conv_w[0], ev_conv_b[0], col_off=EVH_XBC, width=SSD_CONV_DIM)
    cos_t, sin_t = _rotary_tables(n_ctx_rows, seq)
    rdec = jnp.broadcast_to(ev_ret_decay[0][:, :, None], (2, RET_HEADS, LANES)).astype(F32)
    y_ssd, y_ret = _even_scan(xbc, dtp, proj_t, cos_t, sin_t, _dir_lanes(ev_dt_bias[0], 0),
                              _dir_lanes(ev_a_log[0], 0), rdec, n_ctx=n_ctx,
                              rq_blk=EVT_RQ // RET_QK, rk_blk=EVT_RK // RET_QK, rv_blk=EVT_RV // RET_V)
    act = _even_out(y_ssd, y_ret, xbc, proj_h, proj_t, ev_d_skip[0], ev_ssd_norm[0],
                    z_blk=EVH_Z // SSD_INNER, rg_blk=EVT_RG // SSD_INNER)
    o = _matmul(act.reshape(m, act.shape[-1]), [ev_w_out], layer=0, n=d, tn=512, out_dtype=BF16).reshape(b, l, d)
    s, f = _ffn(s, o, mod, 0, norm_mix_post[0], norm_ffn_pre[0], ffn_w1, ffn_w3, ffn_w2)

    s, h = _res_norm(s, mod, o=f, gate=(0, 5), post_w=norm_ffn_post[0], pre_w=norm_mix_pre[1],
                     shift=(1, 0), scale=(1, 1))
    h2 = h.reshape(m, d)
    n_main = 2 * GDN_K + 2 * GDN_V
    proj = _matmul(h2, [od_w_in], layer=0, n=n_main, tn=512, out_dtype=BF16).reshape(b, l, n_main)
    raw = _matmul(h2, [_odd_small_weights(od_w_in[0])], n=2 * LANES, tn=2 * LANES,
                  out_dtype=F32).reshape(b, l, 2 * LANES)
    qk = _conv_silu(proj, od_conv_w[0][:, :2 * GDN_K], None, col_off=0, width=2 * GDN_K,
                    l2_scaled_blocks=GDN_K // CONV_CW, l2_scale=GDN_HEAD_DIM ** -0.5)
    v = _conv_silu(proj, od_conv_w[0][:, 2 * GDN_K:], None, col_off=2 * GDN_K, width=GDN_V)
    o2 = _gdn_scan(qk, v, raw, _dir_lanes(od_dt_bias[0], GDN_V_HEADS), _dir_lanes(od_a_log[0], GDN_V_HEADS),
                   n_ctx=n_ctx)
    ctx_tiles = n_ctx_rows // ROW_TILE
    act = _odd_out(o2, proj, od_norm[0], z_blk=(2 * GDN_K + GDN_V) // GDN_V, row_off=ctx_tiles)
    o = _matmul(act.reshape(b * seq, act.shape[-1]), [od_w_out], layer=0, n=d, tn=512,
                out_dtype=BF16).reshape(b, seq, d)
    s, f = _ffn(s, o, mod, 1, norm_mix_post[1], norm_ffn_pre[1], ffn_w1, ffn_w3, ffn_w2, s_off=ctx_tiles)
    out, _ = _res_norm(s, mod, o=f, gate=(1, 5), post_w=norm_ffn_post[1], has_ctx=False)
    return out
```

```python
import functools
import math

import jax
import jax.numpy as jnp
from jax import lax
from jax.experimental import pallas as pl
from jax.experimental.pallas import tpu as pltpu

F32 = jnp.float32
BF16 = jnp.bfloat16

EPS = 1e-6
CHUNK = 128
LANES = 128
CONV_W = 5
CONV_HALO = 16
GRID_W = 64
ROPE_BASE = 10000.0
ROW_TILE = 256
NEG_BIG = -1e30

SSD_HEADS = 32
SSD_HEAD_DIM = 64
SSD_GROUPS = 4
SSD_STATE = 128
SSD_INNER = SSD_HEADS * SSD_HEAD_DIM
RET_HEADS = 16
RET_QK_DIM = 64
RET_V_DIM = 128
RET_QK = RET_HEADS * RET_QK_DIM
RET_V = RET_HEADS * RET_V_DIM
GDN_K_HEADS = 16
GDN_V_HEADS = 32
GDN_HEAD_DIM = 128
GDN_K = GDN_K_HEADS * GDN_HEAD_DIM
GDN_V = GDN_V_HEADS * GDN_HEAD_DIM

VMEM_LIMIT = 56 * 1024 * 1024


def _cparams(sem):
    return pltpu.CompilerParams(dimension_semantics=sem, vmem_limit_bytes=VMEM_LIMIT)


def _dot(a, b):
    return jnp.dot(a, b, preferred_element_type=F32)


def _dot_nt(a, b):
    return lax.dot_general(a, b, (((1,), (1,)), ((), ())), preferred_element_type=F32)


def _split3(x):
    hi = x.astype(BF16)
    r1 = x - hi.astype(F32)
    mid = r1.astype(BF16)
    lo = (r1 - mid.astype(F32)).astype(BF16)
    return hi, mid, lo


def _dot_sel_left(sel, x):
    hi, mid, lo = _split3(x)
    return _dot(sel, hi) + _dot(sel, mid) + _dot(sel, lo)


def _dot_sel_right(x, sel):
    hi, mid, lo = _split3(x)
    return _dot(hi, sel) + _dot(mid, sel) + _dot(lo, sel)


def _silu(x):
    return x * jax.nn.sigmoid(x)


def _softplus(x):
    return jnp.maximum(x, 0.0) + jnp.log(1.0 + jnp.exp(-jnp.abs(x)))


def _rms(x, w):
    ms = jnp.mean(x * x, axis=-1, keepdims=True)
    return x * lax.rsqrt(ms + EPS) * w


ADA_TN = 512


def _ada_kernel(cb_ref, w_ref, b_ref, o_ref, act_ref, *, n_rows):
    first = jnp.logical_and(pl.program_id(0) == 0, pl.program_id(1) == 0)

    @pl.when(first)
    def _():
        act_ref[...] = _silu(cb_ref[...])

    d = w_ref.shape[1]
    tn = w_ref.shape[2]
    w3 = w_ref[0].reshape(d // 8, 8, tn)
    o_ref[...] = jnp.zeros_like(o_ref)
    for r in range(n_rows):
        a3 = act_ref[r].reshape(d // 8, 8, LANES)
        for lt in range(tn // LANES):
            sl = slice(lt * LANES, (lt + 1) * LANES)
            part = jnp.sum(w3[:, :, sl] * a3, axis=0)
            o_ref[0, r:r + 1, sl] = jnp.sum(part, axis=0, keepdims=True) + b_ref[0, :, sl]


def _ada(cvec, ada_w, ada_b):
    n_rows, d = cvec.shape
    depth, _, n = ada_w.shape
    cb = jnp.broadcast_to(cvec[:, :, None], (n_rows, d, LANES))
    return pl.pallas_call(
        functools.partial(_ada_kernel, n_rows=n_rows),
        grid=(depth, n // ADA_TN),
        in_specs=[
            pl.BlockSpec((n_rows, d, LANES), lambda i, j: (0, 0, 0)),
            pl.BlockSpec((1, d, ADA_TN), lambda i, j: (i, 0, j)),
            pl.BlockSpec((1, 1, ADA_TN), lambda i, j: (i, 0, j)),
        ],
        out_specs=pl.BlockSpec((1, 8, ADA_TN), lambda i, j: (i, 0, j)),
        out_shape=jax.ShapeDtypeStruct((depth, 8, n), F32),
        scratch_shapes=[pltpu.VMEM((n_rows, d, LANES), F32)],
        compiler_params=_cparams(("arbitrary", "arbitrary")),
        name="ada_mod",
    )(cb, ada_w, ada_b.reshape(depth, 1, n))


def _mod_row(m_ref, ctx_row):
    r = pl.program_id(0)
    if ctx_row is not None:
        r = jnp.where(pl.program_id(1) == 0, ctx_row, r)
    return m_ref[0, pl.ds(r, 1), :]


def _res_norm_kernel(*refs, split_in, has_res, has_h, ctx_row):
    it = iter(refs)
    if split_in:
        ctx_ref, x_ref = next(it), next(it)
        s = jnp.where(pl.program_id(1) == 0, ctx_ref[0], x_ref[0])
    else:
        s = next(it)[0]
    if has_res:
        o_ref, gate_ref, post_ref = next(it), next(it), next(it)
    if has_h:
        pre_ref, shift_ref, scale_ref = next(it), next(it), next(it)
    if has_res:
        s = s + _mod_row(gate_ref, ctx_row) * _rms(o_ref[0].astype(F32), post_ref[...])
    if has_res or split_in:
        next(it)[0] = s
    if has_h:
        h = _rms(s, pre_ref[...]) * (1.0 + _mod_row(scale_ref, ctx_row)) + _mod_row(shift_ref, ctx_row)
        next(it)[0] = h.astype(BF16)


def _res_norm(s, mod, *, o=None, gate=None, post_w=None, pre_w=None, shift=None, scale=None,
              s_off=0, o_off=0, n_tiles=None, has_ctx=True):
    split_in = isinstance(s, tuple)
    b, _, d = (s[1] if split_in else s).shape
    has_res = o is not None
    has_h = pre_w is not None
    if n_tiles is None:
        n_tiles = (s[0].shape[1] + s[1].shape[1]) // ROW_TILE if split_in else s.shape[1] // ROW_TILE - s_off
    ctx_row = b if has_ctx else None

    def outrows(bi, j):
        return (bi, j, 0)

    def mod_spec(sel):
        layer, comp = sel
        return pl.BlockSpec((1, 8, d), lambda bi, j: (layer, 0, comp))

    vec_spec = pl.BlockSpec((1, d), lambda bi, j: (0, 0))
    tile = (1, ROW_TILE, d)
    if split_in:
        assert s[0].shape[1] == ROW_TILE and has_ctx and not has_res
        args = list(s)
        in_specs = [pl.BlockSpec(tile, lambda bi, j: (bi, 0, 0)),
                    pl.BlockSpec(tile, lambda bi, j: (bi, jnp.maximum(j - 1, 0), 0))]
    else:
        args, in_specs = [s], [pl.BlockSpec(tile, lambda bi, j: (bi, j + s_off, 0))]
    out_specs, out_shape = [], []
    if has_res:
        args += [o, mod, post_w.reshape(1, d)]
        in_specs += [pl.BlockSpec(tile, lambda bi, j: (bi, j + o_off, 0)), mod_spec(gate), vec_spec]
    if has_res or split_in:
        out_specs.append(pl.BlockSpec(tile, outrows))
        out_shape.append(jax.ShapeDtypeStruct((b, n_tiles * ROW_TILE, d), F32))
    if has_h:
        args += [pre_w.reshape(1, d), mod, mod]
        in_specs += [vec_spec, mod_spec(shift), mod_spec(scale)]
        out_specs.append(pl.BlockSpec(tile, outrows))
        out_shape.append(jax.ShapeDtypeStruct((b, n_tiles * ROW_TILE, d), BF16))
    outs = pl.pallas_call(
        functools.partial(_res_norm_kernel, split_in=split_in, has_res=has_res, has_h=has_h, ctx_row=ctx_row),
        grid=(b, n_tiles),
        in_specs=in_specs,
        out_specs=out_specs,
        out_shape=out_shape,
        compiler_params=_cparams(("parallel", "parallel")),
        name="res_norm",
    )(*args)
    outs = list(outs)
    s_new = outs.pop(0) if (has_res or split_in) else None
    h = outs.pop(0) if has_h else None
    return s_new, h


MM_TM = 2048


def _row_tile(m, cap=MM_TM):
    best = ROW_TILE
    for tm in range(ROW_TILE, cap + 1, ROW_TILE):
        if m % tm == 0:
            best = tm
    return best


def _mm_kernel(a_ref, *refs, swiglu, w_transposed):
    a = a_ref[...]
    dot = _dot_nt if w_transposed else _dot
    if swiglu:
        w1_ref, w3_ref, o_ref = refs
        g = dot(a, w1_ref[0].astype(BF16))
        u = dot(a, w3_ref[0].astype(BF16))
        o_ref[...] = (_silu(g) * u).astype(o_ref.dtype)
    else:
        w_ref, o_ref = refs
        o_ref[...] = dot(a, w_ref[0].astype(BF16)).astype(o_ref.dtype)


def _matmul(a, ws, *, layer=0, col_off=0, n, tn, out_dtype, w_transposed=False):
    m, k = a.shape
    swiglu = len(ws) == 2
    tm = _row_tile(m)
    assert col_off % tn == 0
    off = col_off // tn
    if w_transposed:
        w_spec = pl.BlockSpec((1, tn, k), lambda i, j: (layer, j + off, 0))
    else:
        w_spec = pl.BlockSpec((1, k, tn), lambda i, j: (layer, 0, j + off))
    return pl.pallas_call(
        functools.partial(_mm_kernel, swiglu=swiglu, w_transposed=w_transposed),
        grid=(m // tm, pl.cdiv(n, tn)),
        in_specs=[pl.BlockSpec((tm, k), lambda i, j: (i, 0), pipeline_mode=pl.Buffered(1))]
        + [w_spec] * len(ws),
        out_specs=pl.BlockSpec((tm, tn), lambda i, j: (i, j)),
        out_shape=jax.ShapeDtypeStruct((m, n), out_dtype),
        compiler_params=_cparams(("parallel", "arbitrary")),
        name="matmul_swiglu" if swiglu else "matmul",
    )(a, *ws)


MMK_TN = 1024
MMK_TK = 1024


def _mm_ktiled_kernel(a_ref, w_ref, o_ref, acc_ref, *, k_total, tk):
    kk = pl.program_id(2)
    nk = pl.num_programs(2)

    @pl.when(kk == 0)
    def _():
        acc_ref[...] = jnp.zeros_like(acc_ref)

    rem = k_total - (pl.cdiv(k_total, tk) - 1) * tk

    def step(a, w):
        acc_ref[...] += _dot(a, w.astype(BF16))

    if rem == tk:
        step(a_ref[...], w_ref[0])
    else:
        @pl.when(kk < nk - 1)
        def _():
            step(a_ref[...], w_ref[0])

        @pl.when(kk == nk - 1)
        def _():
            a = a_ref[...]
            w = w_ref[0]
            ka = lax.broadcasted_iota(jnp.int32, a.shape, 1)
            kw = lax.broadcasted_iota(jnp.int32, w.shape, 0)
            step(jnp.where(ka < rem, a, jnp.zeros_like(a)), jnp.where(kw < rem, w, jnp.zeros_like(w)))

    @pl.when(kk == nk - 1)
    def _():
        o_ref[...] = acc_ref[...].astype(o_ref.dtype)


def _matmul_ktiled(a, w, *, layer, tk=MMK_TK):
    m, k = a.shape
    n = w.shape[2]
    tm = _row_tile(m)
    tn = min(MMK_TN, n)
    assert n % tn == 0
    return pl.pallas_call(
        functools.partial(_mm_ktiled_kernel, k_total=k, tk=tk),
        grid=(m // tm, n // tn, pl.cdiv(k, tk)),
        in_specs=[pl.BlockSpec((tm, tk), lambda i, j, kk: (i, kk)),
                  pl.BlockSpec((1, tk, tn), lambda i, j, kk: (layer, kk, j))],
        out_specs=pl.BlockSpec((tm, tn), lambda i, j, kk: (i, j)),
        out_shape=jax.ShapeDtypeStruct((m, n), BF16),
        scratch_shapes=[pltpu.VMEM((tm, tn), F32)],
        compiler_params=_cparams(("parallel", "parallel", "arbitrary")),
        name="matmul_ktiled",
    )(a, w)


CONV_CW = 1024


def _conv_kernel(cur_ref, prev_ref, next_ref, w_ref, *refs, has_bias, l2_scaled_blocks, l2_scale):
    if has_bias:
        b_ref, o_ref = refs
    else:
        (o_ref,) = refs
    j = pl.program_id(1)
    nj = pl.num_programs(1)
    tr = cur_ref.shape[1]
    use_prev = jnp.logical_and(j != 0, j != 1)
    use_next = jnp.logical_and(j != 0, j != nj - 1)
    zeros = jnp.zeros(prev_ref.shape[1:], F32)
    xext = jnp.concatenate([jnp.where(use_prev, prev_ref[0].astype(F32), zeros),
                            cur_ref[0].astype(F32),
                            jnp.where(use_next, next_ref[0].astype(F32), zeros)], axis=0)
    n_ext = xext.shape[0]
    acc = None
    for t in range(CONV_W):
        shift = (CONV_W // 2 - t) % n_ext
        tap = pltpu.roll(xext, shift, axis=0) if shift else xext
        term = w_ref[t:t + 1, :] * tap[CONV_HALO:CONV_HALO + tr, :]
        acc = term if acc is None else acc + term
    if has_bias:
        acc = acc + b_ref[...]
    y = _silu(acc)
    if l2_scaled_blocks is None:
        o_ref[0] = y.astype(o_ref.dtype)
    else:
        scale = jnp.where(pl.program_id(2) < l2_scaled_blocks, l2_scale, 1.0).astype(F32)
        for hd in range(y.shape[1] // LANES):
            sl = slice(hd * LANES, (hd + 1) * LANES)
            yh = y[:, sl]
            ss = jnp.sum(yh * yh, axis=-1, keepdims=True)
            o_ref[0, :, sl] = (yh * (lax.rsqrt(ss + EPS) * scale)).astype(o_ref.dtype)


def _conv_silu(proj, conv_w, conv_b, *, col_off, width, l2_scaled_blocks=None, l2_scale=1.0):
    b, l, _ = proj.shape
    assert col_off % CONV_CW == 0 and width % CONV_CW == 0 and l % ROW_TILE == 0
    coff = col_off // CONV_CW
    hpt = ROW_TILE // CONV_HALO
    n_halo = l // CONV_HALO
    has_bias = conv_b is not None
    args = [proj, proj, proj, conv_w]
    in_specs = [
        pl.BlockSpec((1, ROW_TILE, CONV_CW), lambda bi, j, c: (bi, j, c + coff)),
        pl.BlockSpec((1, CONV_HALO, CONV_CW), lambda bi, j, c: (bi, jnp.maximum(j * hpt - 1, 0), c + coff)),
        pl.BlockSpec((1, CONV_HALO, CONV_CW),
                     lambda bi, j, c: (bi, jnp.minimum((j + 1) * hpt, n_halo - 1), c + coff)),
        pl.BlockSpec((CONV_W, CONV_CW), lambda bi, j, c: (0, c)),
    ]
    if has_bias:
        args.append(conv_b.reshape(1, width))
        in_specs.append(pl.BlockSpec((1, CONV_CW), lambda bi, j, c: (0, c)))
    return pl.pallas_call(
        functools.partial(_conv_kernel, has_bias=has_bias, l2_scaled_blocks=l2_scaled_blocks, l2_scale=l2_scale),
        grid=(b, l // ROW_TILE, width // CONV_CW),
        in_specs=in_specs,
        out_specs=pl.BlockSpec((1, ROW_TILE, CONV_CW), lambda bi, j, c: (bi, j, c)),
        out_shape=jax.ShapeDtypeStruct((b, l, width), BF16),
        compiler_params=_cparams(("parallel", "parallel", "parallel")),
        name="conv_silu",
    )(*args)


def _chunk_of(d, t, n_ctx, n_all):
    rev = jnp.where(t < n_ctx, n_ctx - 1 - t, n_all - 1 - (t - n_ctx))
    return jnp.where(d == 0, t, rev)


def _scan_masks(d, reps=1):
    ii = lax.broadcasted_iota(jnp.int32, (CHUNK, reps * CHUNK), 0)
    jj = lax.broadcasted_iota(jnp.int32, (CHUNK, reps * CHUNK), 1) % CHUNK
    fwd = d == 0
    later = jnp.where(fwd, ii, jj)
    earlier = jnp.where(fwd, jj, ii)
    incl = earlier <= later
    strict = earlier < later
    return ii, jj, fwd, incl, strict


def _even_scan_kernel(xs_ref, bc_ref, dt_ref, rq_ref, rk_ref, rv_ref, cos_ref, sin_ref,
                      dtb_ref, alog_ref, rdec_ref, yssd_ref, yret_ref, hs_ref, hr_ref):
    d = pl.program_id(1)
    t = pl.program_id(2)

    @pl.when(t == 0)
    def _():
        hs_ref[...] = jnp.zeros_like(hs_ref)
        hr_ref[...] = jnp.zeros_like(hr_ref)

    ii, jj, fwd, incl, _ = _scan_masks(d)
    lane = lax.broadcasted_iota(jnp.int32, (CHUNK, LANES), 1)
    row = lax.broadcasted_iota(jnp.int32, (CHUNK, LANES), 0)
    lane_lo = lane < (LANES // 2)
    row_lo = row < (CHUNK // 2)

    dtf = _softplus(dt_ref[0] + dtb_ref[0])
    la = -jnp.exp(alog_ref[0]) * dtf
    tri = jnp.where(incl, 1.0, 0.0).astype(BF16)
    cs = _dot_sel_left(tri, la)
    tot = jnp.sum(la, axis=0, keepdims=True)
    cs_t = cs.T
    dt_t = dtf.T
    wend_t = (dtf * jnp.exp(tot - cs)).T

    bc = bc_ref[0].astype(F32)
    n_state = SSD_GROUPS * SSD_STATE
    for g in range(SSD_GROUPS):
        bm = bc[:, g * SSD_STATE:(g + 1) * SSD_STATE]
        cm = bc[:, n_state + g * SSD_STATE:n_state + (g + 1) * SSD_STATE]
        bm_bf = bm.astype(BF16)
        cm_bf = cm.astype(BF16)
        scores = _dot_nt(cm_bf, bm_bf)
        bm_t = bm.T
        heads_per_group = SSD_HEADS // SSD_GROUPS
        for pp in range(heads_per_group // 2):
            pidx = g * (heads_per_group // 2) + pp
            xs_pair = xs_ref[0, :, pidx * LANES:(pidx + 1) * LANES]
            x_lo = jnp.where(lane_lo, xs_pair, 0.0).astype(BF16)
            x_hi = jnp.where(lane_lo, 0.0, xs_pair).astype(BF16)
            h_pair = hs_ref[pidx]
            h_lo = jnp.where(lane_lo, h_pair, 0.0).astype(BF16)
            h_hi = jnp.where(lane_lo, 0.0, h_pair).astype(BF16)
            ms, qs, kws, ets = [], [], [], []
            for r in (2 * pidx, 2 * pidx + 1):
                cs_b = jnp.broadcast_to(cs[:, r:r + 1], (CHUNK, LANES))
                dec = jnp.exp(jnp.where(incl, cs_b - cs_t[r:r + 1, :], NEG_BIG))
                ms.append((scores * dec * dt_t[r:r + 1, :]).astype(BF16))
                qs.append((cm * jnp.exp(cs_b)).astype(BF16))
                kws.append((bm_t * wend_t[r:r + 1, :]).astype(BF16))
                ets.append(jnp.exp(jnp.where(fwd, cs_b[CHUNK - 1:CHUNK, :], cs_b[0:1, :])))
            lhs = jnp.concatenate(ms + qs, axis=1)
            rhs = jnp.concatenate([x_lo, x_hi, h_lo, h_hi], axis=0)
            yssd_ref[0, 0, :, pidx * LANES:(pidx + 1) * LANES] = _dot(lhs, rhs).astype(yssd_ref.dtype)
            et = jnp.where(lane_lo, jnp.broadcast_to(ets[0], (CHUNK, LANES)),
                           jnp.broadcast_to(ets[1], (CHUNK, LANES)))
            upd = _dot(jnp.concatenate(kws, axis=1), jnp.concatenate([x_lo, x_hi], axis=0))
            hs_ref[pidx] = h_pair * et + upd

    lg = jax.nn.log_sigmoid(rdec_ref[0])
    dist = jnp.where(fwd, ii - jj, jj - ii).astype(F32)
    cnt = jnp.where(fwd, ii + 1, CHUNK - ii).astype(F32)
    rem_t = jnp.where(fwd, CHUNK - 1 - jj, jj).astype(F32)
    cos = cos_ref[...]
    sin = sin_ref[...]
    half = RET_QK_DIM // 2
    swap_lo = (lane % RET_QK_DIM) < half

    def rotary(tt):
        swapped = jnp.where(swap_lo, pltpu.roll(tt, LANES - half, axis=1), pltpu.roll(tt, half, axis=1))
        return tt * cos + swapped * sin

    for p in range(RET_HEADS // 2):
        sl = slice(p * LANES, (p + 1) * LANES)
        q_pair = rotary(rq_ref[0, :, sl].astype(F32))
        k_pair = rotary(rk_ref[0, :, sl].astype(F32) * (RET_QK_DIM ** -0.5))
        k_pair_bf = k_pair.astype(BF16)
        k_t = k_pair.T
        h_pair = hr_ref[p]
        h_bf = h_pair.astype(BF16)
        kts, vs, es = [], [], []
        for side, hh in enumerate((2 * p, 2 * p + 1)):
            lgb = jnp.broadcast_to(lg[hh:hh + 1, :], (CHUNK, LANES))
            keep = lane_lo if side == 0 else jnp.logical_not(lane_lo)
            keep_rows = row_lo if side == 0 else jnp.logical_not(row_lo)
            q_h = jnp.where(keep, q_pair, 0.0)
            scores = _dot_nt(q_h.astype(BF16), k_pair_bf)
            dec = jnp.where(incl, jnp.exp(lgb * dist), 0.0)
            m = (scores * dec).astype(BF16)
            qe = (q_h * jnp.exp(lgb * cnt)).astype(BF16)
            v_h = rv_ref[0, :, hh * RET_V_DIM:(hh + 1) * RET_V_DIM].astype(BF16)
            y = _dot(jnp.concatenate([m, qe], axis=1), jnp.concatenate([v_h, h_bf], axis=0))
            yret_ref[0, 0, :, hh * RET_V_DIM:(hh + 1) * RET_V_DIM] = y.astype(yret_ref.dtype)
            kts.append(jnp.where(keep_rows, k_t * jnp.exp(lgb * rem_t), 0.0).astype(BF16))
            vs.append(v_h)
            es.append(jnp.exp(lgb * float(CHUNK)))
        upd = _dot(jnp.concatenate(kts, axis=1), jnp.concatenate(vs, axis=0))
        hr_ref[p] = h_pair * jnp.where(row_lo, es[0], es[1]) + upd


def _even_scan(xbc, dtp, proj, cos_t, sin_t, dt_bias, a_log, ret_decay, *, n_ctx, rq_blk, rk_blk, rv_blk):
    b, l, _ = xbc.shape
    nc = l // CHUNK
    cmap = functools.partial(_chunk_of, n_ctx=n_ctx, n_all=nc)
    out_spec = pl.BlockSpec((1, 1, CHUNK, SSD_INNER), lambda bi, d, t: (bi, d, cmap(d, t), 0))
    return pl.pallas_call(
        _even_scan_kernel,
        grid=(b, 2, nc),
        in_specs=[
            pl.BlockSpec((1, CHUNK, SSD_INNER), lambda bi, d, t: (bi, cmap(d, t), 0)),
            pl.BlockSpec((1, CHUNK, 2 * SSD_GROUPS * SSD_STATE), lambda bi, d, t: (bi, cmap(d, t), 2)),
            pl.BlockSpec((1, CHUNK, LANES), lambda bi, d, t: (bi, cmap(d, t), d)),
            pl.BlockSpec((1, CHUNK, RET_QK), lambda bi, d, t: (bi, cmap(d, t), rq_blk)),
            pl.BlockSpec((1, CHUNK, RET_QK), lambda bi, d, t: (bi, cmap(d, t), rk_blk)),
            pl.BlockSpec((1, CHUNK, RET_V), lambda bi, d, t: (bi, cmap(d, t), rv_blk)),
            pl.BlockSpec((CHUNK, LANES), lambda bi, d, t: (cmap(d, t), 0)),
            pl.BlockSpec((CHUNK, LANES), lambda bi, d, t: (cmap(d, t), 0)),
            pl.BlockSpec((1, 1, LANES), lambda bi, d, t: (d, 0, 0)),
            pl.BlockSpec((1, 1, LANES), lambda bi, d, t: (d, 0, 0)),
            pl.BlockSpec((1, RET_HEADS, LANES), lambda bi, d, t: (d, 0, 0)),
        ],
        out_specs=[out_spec, out_spec],
        out_shape=[jax.ShapeDtypeStruct((b, 2, l, SSD_INNER), BF16),
                   jax.ShapeDtypeStruct((b, 2, l, RET_V), BF16)],
        scratch_shapes=[pltpu.VMEM((SSD_HEADS // 2, SSD_STATE, LANES), F32),
                        pltpu.VMEM((RET_HEADS // 2, LANES, RET_V_DIM), F32)],
        compiler_params=_cparams(("parallel", "parallel", "arbitrary")),
        name="even_scan",
    )(xbc, xbc, dtp, proj, proj, proj, cos_t, sin_t, dt_bias, a_log, ret_decay)


def _even_out_kernel(ys0_ref, ys1_ref, yr0_ref, yr1_ref, xs_ref, z_ref, rg_ref, dsk_ref, nw_ref, o_ref):
    def f32(ref, *idx):
        return ref[idx].astype(F32)

    y = (f32(ys0_ref, 0, 0) + f32(ys1_ref, 0, 0) + dsk_ref[...] * f32(xs_ref, 0)) * _silu(f32(z_ref, 0))
    gsz = SSD_INNER // SSD_GROUPS
    for g in range(SSD_GROUPS):
        sl = slice(g * gsz, (g + 1) * gsz)
        o_ref[0, :, sl] = _rms(y[:, sl], nw_ref[:, sl]).astype(BF16)
    yr = f32(yr0_ref, 0, 0) + f32(yr1_ref, 0, 0)
    gate = _silu(f32(rg_ref, 0))
    for hh in range(RET_HEADS):
        sl = slice(hh * RET_V_DIM, (hh + 1) * RET_V_DIM)
        v = yr[:, sl]
        mu = jnp.mean(v, axis=-1, keepdims=True)
        cen = v - mu
        var = jnp.mean(cen * cen, axis=-1, keepdims=True)
        o_ref[0, :, SSD_INNER + hh * RET_V_DIM:SSD_INNER + (hh + 1) * RET_V_DIM] = (
            cen * lax.rsqrt(var + EPS) * gate[:, sl]).astype(BF16)


EVEN_OUT_TILE = ROW_TILE


def _even_out(y_ssd, y_ret, xbc, z_src, rg_src, d_skip, ssd_norm, *, z_blk, rg_blk):
    b, _, l, _ = y_ssd.shape
    tr = EVEN_OUT_TILE
    w = SSD_INNER
    dsk = jnp.repeat(d_skip, SSD_HEAD_DIM).reshape(1, w)

    def dir_spec(dd):
        return pl.BlockSpec((1, 1, tr, w), lambda bi, j: (bi, dd, j, 0))

    def col_spec(blk):
        return pl.BlockSpec((1, tr, w), lambda bi, j: (bi, j, blk))

    vec = pl.BlockSpec((1, w), lambda bi, j: (0, 0))
    return pl.pallas_call(
        _even_out_kernel,
        grid=(b, l // tr),
        in_specs=[dir_spec(0), dir_spec(1), dir_spec(0), dir_spec(1), col_spec(0), col_spec(z_blk),
                  col_spec(rg_blk), vec, vec],
        out_specs=pl.BlockSpec((1, tr, 2 * w), lambda bi, j: (bi, j, 0)),
        out_shape=jax.ShapeDtypeStruct((b, l, 2 * w), BF16),
        compiler_params=_cparams(("parallel", "parallel")),
        name="even_out",
    )(y_ssd, y_ssd, y_ret, y_ret, xbc, z_src, rg_src, dsk, ssd_norm.reshape(1, w))


GDN_LEVELS = 7
GDN_GROUP = 16


def _bdot(a, b):
    return jnp.einsum("gmk,gkn->gmn", a, b, preferred_element_type=F32)


def _bdot_nt(a, b):
    return jnp.einsum("gmk,gnk->gmn", a, b, preferred_element_type=F32)


def _gdn_scan_kernel(q_ref, k_ref, v_ref, raw_ref, bias_ref, alog_ref, lvl_ref, o_ref, s_ref):
    d = pl.program_id(1)
    t = pl.program_id(2)

    @pl.when(t == 0)
    def _():
        s_ref[...] = jnp.zeros_like(s_ref)

    ii, jj, fwd, incl, strict = _scan_masks(d)
    strict_f = jnp.where(strict, 1.0, 0.0)
    eye = jnp.where(ii == jj, 1.0, 0.0)
    tri = jnp.where(incl, 1.0, 0.0).astype(BF16)

    raw = raw_ref[0]
    beta = jax.nn.sigmoid(raw)
    gl = -jnp.exp(alog_ref[0]) * _softplus(raw + bias_ref[0])
    cs = _dot_sel_left(tri, gl)
    tot = jnp.sum(gl, axis=0, keepdims=True)
    cs_t = cs.T
    wend_t = jnp.exp(tot - cs).T

    gsz = GDN_GROUP
    rep = GDN_V_HEADS // GDN_K_HEADS
    kper = gsz // rep
    hd = GDN_HEAD_DIM
    strict_f3 = strict_f[None]
    incl3 = incl[None]

    def col_bcast(a, c):
        return jnp.broadcast_to(a[:, c:c + 1], (CHUNK, LANES))

    def per_key_head(a4, x):
        res = a4[:, None] * x.reshape((kper, rep) + x.shape[1:])
        return res.reshape((gsz,) + res.shape[2:])

    for gi in range(GDN_V_HEADS // gsz):
        heads = range(gi * gsz, (gi + 1) * gsz)
        kheads = range(gi * kper, (gi + 1) * kper)
        q4b = jnp.stack([q_ref[0, :, kh * hd:(kh + 1) * hd] for kh in kheads])
        k4b = jnp.stack([k_ref[0, :, kh * hd:(kh + 1) * hd] for kh in kheads])
        k4 = k4b.astype(F32)
        kk4 = _bdot_nt(k4b, k4b)
        qk4 = _bdot_nt(q4b, k4b)
        kt4 = jnp.stack([k4[i].T for i in range(kper)])
        v = jnp.stack([v_ref[0, :, h * hd:(h + 1) * hd] for h in heads])
        cs_b = jnp.stack([col_bcast(cs, GDN_V_HEADS + h) for h in heads])
        beta_b = jnp.stack([col_bcast(beta, h) for h in heads])
        cs_r = jnp.stack([cs_t[GDN_V_HEADS + h:GDN_V_HEADS + h + 1, :] for h in heads])
        wend_r = jnp.stack([wend_t[GDN_V_HEADS + h:GDN_V_HEADS + h + 1, :] for h in heads])
        gam = jnp.exp(jnp.where(incl3, cs_b - cs_r, NEG_BIG))
        ecs_b = jnp.exp(cs_b)
        etot = jnp.exp(jnp.where(fwd, cs_b[:, CHUNK - 1:CHUNK, :], cs_b[:, 0:1, :]))
        low = per_key_head(kk4, gam * (strict_f3 * beta_b))
        tinv = eye[None] - low * lvl_ref[0][None]
        for lv in range(1, GDN_LEVELS):
            a_off = (low * lvl_ref[lv][None]).astype(BF16)
            x = _bdot(a_off, tinv.astype(BF16))
            tinv = tinv - _bdot(tinv.astype(BF16), x.astype(BF16))
        state = s_ref[gi * gsz:(gi + 1) * gsz]
        ke = per_key_head(k4, ecs_b).astype(BF16)
        qe = per_key_head(q4b.astype(F32), ecs_b).astype(BF16)
        kq_s = _bdot(jnp.concatenate([ke, qe], axis=1), state.astype(BF16))
        resid = (beta_b * (v.astype(F32) - kq_s[:, :CHUNK])).astype(BF16)
        v_new_bf = _bdot(tinv.astype(BF16), resid).astype(BF16)
        out = kq_s[:, CHUNK:] + _bdot(per_key_head(qk4, gam).astype(BF16), v_new_bf)
        for g, h in enumerate(heads):
            o_ref[0, 0, :, h * hd:(h + 1) * hd] = out[g].astype(o_ref.dtype)
        k_end_t = per_key_head(kt4, wend_r).astype(BF16)
        s_ref[gi * gsz:(gi + 1) * gsz] = state * etot + _bdot(k_end_t, v_new_bf)


def _gdn_level_masks():
    i = jnp.arange(CHUNK)[:, None]
    j = jnp.arange(CHUNK)[None, :]
    masks = []
    for lv in range(GDN_LEVELS):
        blk = 1 << lv
        masks.append(((i // (2 * blk) == j // (2 * blk)) & (i // blk != j // blk)).astype(F32))
    return jnp.stack(masks)


def _gdn_scan(qk, v, raw, bias, a_log, *, n_ctx):
    b, l, _ = qk.shape
    nc = l // CHUNK
    cmap = functools.partial(_chunk_of, n_ctx=n_ctx, n_all=nc)
    return pl.pallas_call(
        _gdn_scan_kernel,
        grid=(b, 2, nc),
        in_specs=[
            pl.BlockSpec((1, CHUNK, GDN_K), lambda bi, d, t: (bi, cmap(d, t), 0)),
            pl.BlockSpec((1, CHUNK, GDN_K), lambda bi, d, t: (bi, cmap(d, t), 1)),
            pl.BlockSpec((1, CHUNK, GDN_V), lambda bi, d, t: (bi, cmap(d, t), 0)),
            pl.BlockSpec((1, CHUNK, LANES), lambda bi, d, t: (bi, cmap(d, t), d)),
            pl.BlockSpec((1, 1, LANES), lambda bi, d, t: (d, 0, 0)),
            pl.BlockSpec((1, 1, LANES), lambda bi, d, t: (d, 0, 0)),
            pl.BlockSpec((GDN_LEVELS, CHUNK, CHUNK), lambda bi, d, t: (0, 0, 0)),
        ],
        out_specs=pl.BlockSpec((1, 1, CHUNK, GDN_V), lambda bi, d, t: (bi, d, cmap(d, t), 0)),
        out_shape=jax.ShapeDtypeStruct((b, 2, l, GDN_V), BF16),
        scratch_shapes=[pltpu.VMEM((GDN_V_HEADS, GDN_HEAD_DIM, GDN_HEAD_DIM), F32)],
        compiler_params=_cparams(("parallel", "parallel", "arbitrary")),
        name="gdn_scan",
    )(qk, qk, v, raw, bias, a_log, _gdn_level_masks())


def _odd_out_kernel(o0_ref, o1_ref, z_ref, nw_ref, out_ref):
    o = o0_ref[0, 0].astype(F32) + o1_ref[0, 0].astype(F32)
    gate = _silu(z_ref[0].astype(F32))
    for hh in range(GDN_V_HEADS):
        sl = slice(hh * GDN_HEAD_DIM, (hh + 1) * GDN_HEAD_DIM)
        out_ref[0, :, sl] = (_rms(o[:, sl], nw_ref[...]) * gate[:, sl]).astype(BF16)


def _odd_out(o, proj, norm_w, *, z_blk, row_off):
    b, _, l, w = o.shape
    tr = ROW_TILE
    n_tiles = l // tr - row_off

    def dir_spec(dd):
        return pl.BlockSpec((1, 1, tr, w), lambda bi, j: (bi, dd, j + row_off, 0))

    return pl.pallas_call(
        _odd_out_kernel,
        grid=(b, n_tiles),
        in_specs=[dir_spec(0), dir_spec(1),
                  pl.BlockSpec((1, tr, w), lambda bi, j: (bi, j + row_off, z_blk)),
                  pl.BlockSpec((1, GDN_HEAD_DIM), lambda bi, j: (0, 0))],
        out_specs=pl.BlockSpec((1, tr, w), lambda bi, j: (bi, j, 0)),
        out_shape=jax.ShapeDtypeStruct((b, n_tiles * tr, w), BF16),
        compiler_params=_cparams(("parallel", "parallel")),
        name="odd_out",
    )(o, o, proj, norm_w.reshape(1, GDN_HEAD_DIM))


def _dir_lanes(p, lane_off):
    out = jnp.zeros((2, 1, LANES), F32)
    return out.at[:, 0, lane_off:lane_off + p.shape[1]].set(p.astype(F32))


def _rotary_tables(n_ctx_rows, n_lat_rows):
    nf = RET_QK_DIM // 4
    pos = jnp.arange(n_lat_rows)
    rowp = (pos // GRID_W).astype(F32)
    colp = (pos % GRID_W).astype(F32)
    freqs = ROPE_BASE ** (-jnp.arange(nf, dtype=F32) / nf)
    ang = jnp.concatenate([rowp[:, None] * freqs, colp[:, None] * freqs], axis=-1)
    cos = jnp.cos(ang)
    sin = jnp.sin(ang)
    cos_h = jnp.concatenate([cos, cos], axis=-1)
    sin_h = jnp.concatenate([-sin, sin], axis=-1)
    reps = LANES // RET_QK_DIM
    cos_t = jnp.concatenate([jnp.ones((n_ctx_rows, LANES), F32), jnp.tile(cos_h, (1, reps))], axis=0)
    sin_t = jnp.concatenate([jnp.zeros((n_ctx_rows, LANES), F32), jnp.tile(sin_h, (1, reps))], axis=0)
    return cos_t, sin_t


SSD_CONV_DIM = SSD_INNER + 2 * SSD_GROUPS * SSD_STATE
EV_HEAD = SSD_INNER + SSD_CONV_DIM
EV_TAIL_OFF = EV_HEAD + 2 * SSD_HEADS
EV_TAIL = 2 * RET_QK + 2 * RET_V
EVH_Z, EVH_XBC = 0, SSD_INNER
EVT_RQ, EVT_RK, EVT_RV, EVT_RG = 0, RET_QK, 2 * RET_QK, 2 * RET_QK + RET_V


def _even_weights(w_in_t):
    tail = w_in_t[EV_TAIL_OFF:EV_TAIL_OFF + EV_TAIL].astype(BF16)
    pad = jnp.zeros((LANES - SSD_HEADS, w_in_t.shape[1]), w_in_t.dtype)
    dt = w_in_t[EV_HEAD:EV_TAIL_OFF]
    w_dt = jnp.concatenate([dt[:SSD_HEADS], pad, dt[SSD_HEADS:], pad], axis=0)
    return tail[None], w_dt[None]


def _odd_small_weights(w_in):
    base = 2 * GDN_K + 2 * GDN_V
    hv = GDN_V_HEADS
    beta = w_in[:, base:base + 2 * hv]
    dec = w_in[:, base + 2 * hv:base + 4 * hv]
    pad = jnp.zeros((w_in.shape[0], LANES - 2 * hv), w_in.dtype)
    return jnp.concatenate([beta[:, :hv], dec[:, :hv], pad, beta[:, hv:], dec[:, hv:], pad], axis=1)[None]


def _ffn(s, o, mod, layer, post_mix, pre_ffn, w1, w3, w2, *, s_off=0):
    b, l, d = o.shape
    s1, h = _res_norm(s, mod, o=o, gate=(layer, 2), post_w=post_mix, pre_w=pre_ffn,
                      shift=(layer, 3), scale=(layer, 4), s_off=s_off, n_tiles=l // ROW_TILE,
                      has_ctx=s_off == 0)
    d_ff = w1.shape[2]
    act = _matmul(h.reshape(b * l, d), [w1, w3], layer=layer, n=d_ff, tn=256, out_dtype=BF16)
    f = _matmul_ktiled(act, w2, layer=layer)
    return s1, f.reshape(b, l, d)


def kernel(x, c, ctx, c_ctx, ada_w, ada_b, norm_mix_pre, norm_mix_post, norm_ffn_pre, norm_ffn_post,
           ev_w_in, ev_conv_w, ev_conv_b, ev_dt_bias, ev_a_log, ev_d_skip, ev_ssd_norm, ev_ret_decay,
           ev_w_out, od_w_in, od_conv_w, od_dt_bias, od_a_log, od_norm, od_w_out,
           ffn_w1, ffn_w3, ffn_w2):
    b, seq, d = x.shape
    n_ctx_rows = ctx.shape[1]
    assert n_ctx_rows == ROW_TILE and seq % ROW_TILE == 0
    l = n_ctx_rows + seq
    m = b * l
    n_ctx = n_ctx_rows // CHUNK

    mod = _ada(jnp.concatenate([c, c_ctx[None]], axis=0), ada_w, ada_b)

    s, h = _res_norm((ctx, x), mod, pre_w=norm_mix_pre[0], shift=(0, 0), scale=(0, 1))
    ev_w_in_t = jnp.swapaxes(ev_w_in, 1, 2)
    w_tail, w_dt = _even_weights(ev_w_in_t[0])
    h2 = h.reshape(m, d)
    proj_h = _matmul(h2, [ev_w_in_t], layer=0, n=EV_HEAD, tn=512, out_dtype=BF16,
                     w_transposed=True).reshape(b, l, EV_HEAD)
    proj_t = _matmul(h2, [w_tail], n=EV_TAIL, tn=1024, out_dtype=BF16, w_transposed=True).reshape(b, l, EV_TAIL)
    dtp = _matmul(h2, [w_dt], n=2 * LANES, tn=2 * LANES, out_dtype=F32,
                  w_transposed=True).reshape(b, l, 2 * LANES)
    xbc = _conv_silu(proj_h, ev_conv_w[0], ev_conv_b[0], col_off=EVH_XBC, width=SSD_CONV_DIM)
    cos_t, sin_t = _rotary_tables(n_ctx_rows, seq)
    rdec = jnp.broadcast_to(ev_ret_decay[0][:, :, None], (2, RET_HEADS, LANES)).astype(F32)
    y_ssd, y_ret = _even_scan(xbc, dtp, proj_t, cos_t, sin_t, _dir_lanes(ev_dt_bias[0], 0),
                              _dir_lanes(ev_a_log[0], 0), rdec, n_ctx=n_ctx,
                              rq_blk=EVT_RQ // RET_QK, rk_blk=EVT_RK // RET_QK, rv_blk=EVT_RV // RET_V)
    act = _even_out(y_ssd, y_ret, xbc, proj_h, proj_t, ev_d_skip[0], ev_ssd_norm[0],
                    z_blk=EVH_Z // SSD_INNER, rg_blk=EVT_RG // SSD_INNER)
    o = _matmul(act.reshape(m, act.shape[-1]), [ev_w_out], layer=0, n=d, tn=512, out_dtype=BF16).reshape(b, l, d)
    s, f = _ffn(s, o, mod, 0, norm_mix_post[0], norm_ffn_pre[0], ffn_w1, ffn_w3, ffn_w2)

    s, h = _res_norm(s, mod, o=f, gate=(0, 5), post_w=norm_ffn_post[0], pre_w=norm_mix_pre[1],
                     shift=(1, 0), scale=(1, 1))
    h2 = h.reshape(m, d)
    n_main = 2 * GDN_K + 2 * GDN_V
    proj = _matmul(h2, [od_w_in], layer=0, n=n_main, tn=512, out_dtype=BF16).reshape(b, l, n_main)
    raw = _matmul(h2, [_odd_small_weights(od_w_in[0])], n=2 * LANES, tn=2 * LANES,
                  out_dtype=F32).reshape(b, l, 2 * LANES)
    qk = _conv_silu(proj, od_conv_w[0][:, :2 * GDN_K], None, col_off=0, width=2 * GDN_K,
                    l2_scaled_blocks=GDN_K // CONV_CW, l2_scale=GDN_HEAD_DIM ** -0.5)
    v = _conv_silu(proj, od_conv_w[0][:, 2 * GDN_K:], None, col_off=2 * GDN_K, width=GDN_V)
    o2 = _gdn_scan(qk, v, raw, _dir_lanes(od_dt_bias[0], GDN_V_HEADS), _dir_lanes(od_a_log[0], GDN_V_HEADS),
                   n_ctx=n_ctx)
    ctx_tiles = n_ctx_rows // ROW_TILE
    act = _odd_out(o2, proj, od_norm[0], z_blk=(2 * GDN_K + GDN_V) // GDN_V, row_off=ctx_tiles)
    o = _matmul(act.reshape(b * seq, act.shape[-1]), [od_w_out], layer=0, n=d, tn=512,
                out_dtype=BF16).reshape(b, seq, d)
    s, f = _ffn(s, o, mod, 1, norm_mix_post[1], norm_ffn_pre[1], ffn_w1, ffn_w3, ffn_w2, s_off=ctx_tiles)
    out, _ = _res_norm(s, mod, o=f, gate=(1, 5), post_w=norm_ffn_post[1], has_ctx=False)
    return out
```

```python
import functools
import math

import jax
import jax.numpy as jnp
from jax import lax
from jax.experimental import pallas as pl
from jax.experimental.pallas import tpu as pltpu

F32 = jnp.float32
BF16 = jnp.bfloat16

EPS = 1e-6
CHUNK = 128
LANES = 128
CONV_W = 5
CONV_HALO = 16
GRID_W = 64
ROPE_BASE = 10000.0
ROW_TILE = 256
NEG_BIG = -1e30

SSD_HEADS = 32
SSD_HEAD_DIM = 64
SSD_GROUPS = 4
SSD_STATE = 128
SSD_INNER = SSD_HEADS * SSD_HEAD_DIM
RET_HEADS = 16
RET_QK_DIM = 64
RET_V_DIM = 128
RET_QK = RET_HEADS * RET_QK_DIM
RET_V = RET_HEADS * RET_V_DIM
GDN_K_HEADS = 16
GDN_V_HEADS = 32
GDN_HEAD_DIM = 128
GDN_K = GDN_K_HEADS * GDN_HEAD_DIM
GDN_V = GDN_V_HEADS * GDN_HEAD_DIM

VMEM_LIMIT = 56 * 1024 * 1024


def _cparams(sem):
    return pltpu.CompilerParams(dimension_semantics=sem, vmem_limit_bytes=VMEM_LIMIT)


def _dot(a, b):
    return jnp.dot(a, b, preferred_element_type=F32)


def _dot_nt(a, b):
    return lax.dot_general(a, b, (((1,), (1,)), ((), ())), preferred_element_type=F32)


def _split3(x):
    hi = x.astype(BF16)
    r1 = x - hi.astype(F32)
    mid = r1.astype(BF16)
    lo = (r1 - mid.astype(F32)).astype(BF16)
    return hi, mid, lo


def _dot_sel_left(sel, x):
    hi, mid, lo = _split3(x)
    return _dot(sel, hi) + _dot(sel, mid) + _dot(sel, lo)


def _dot_sel_right(x, sel):
    hi, mid, lo = _split3(x)
    return _dot(hi, sel) + _dot(mid, sel) + _dot(lo, sel)


def _silu(x):
    return x * jax.nn.sigmoid(x)


def _softplus(x):
    return jnp.maximum(x, 0.0) + jnp.log(1.0 + jnp.exp(-jnp.abs(x)))


def _rms(x, w):
    ms = jnp.mean(x * x, axis=-1, keepdims=True)
    return x * lax.rsqrt(ms + EPS) * w


ADA_TN = 512


def _ada_kernel(cb_ref, w_ref, b_ref, o_ref, act_ref, *, n_rows):
    first = jnp.logical_and(pl.program_id(0) == 0, pl.program_id(1) == 0)

    @pl.when(first)
    def _():
        act_ref[...] = _silu(cb_ref[...])

    d = w_ref.shape[1]
    tn = w_ref.shape[2]
    w3 = w_ref[0].reshape(d // 8, 8, tn)
    o_ref[...] = jnp.zeros_like(o_ref)
    for r in range(n_rows):
        a3 = act_ref[r].reshape(d // 8, 8, LANES)
        for lt in range(tn // LANES):
            sl = slice(lt * LANES, (lt + 1) * LANES)
            part = jnp.sum(w3[:, :, sl] * a3, axis=0)
            o_ref[0, r:r + 1, sl] = jnp.sum(part, axis=0, keepdims=True) + b_ref[0, :, sl]


def _ada(cvec, ada_w, ada_b):
    n_rows, d = cvec.shape
    depth, _, n = ada_w.shape
    cb = jnp.broadcast_to(cvec[:, :, None], (n_rows, d, LANES))
    return pl.pallas_call(
        functools.partial(_ada_kernel, n_rows=n_rows),
        grid=(depth, n // ADA_TN),
        in_specs=[
            pl.BlockSpec((n_rows, d, LANES), lambda i, j: (0, 0, 0)),
            pl.BlockSpec((1, d, ADA_TN), lambda i, j: (i, 0, j)),
            pl.BlockSpec((1, 1, ADA_TN), lambda i, j: (i, 0, j)),
        ],
        out_specs=pl.BlockSpec((1, 8, ADA_TN), lambda i, j: (i, 0, j)),
        out_shape=jax.ShapeDtypeStruct((depth, 8, n), F32),
        scratch_shapes=[pltpu.VMEM((n_rows, d, LANES), F32)],
        compiler_params=_cparams(("arbitrary", "arbitrary")),
        name="ada_mod",
    )(cb, ada_w, ada_b.reshape(depth, 1, n))


def _mod_row(m_ref, ctx_row):
    r = pl.program_id(0)
    if ctx_row is not None:
        r = jnp.where(pl.program_id(1) == 0, ctx_row, r)
    return m_ref[0, pl.ds(r, 1), :]


def _res_norm_kernel(*refs, split_in, has_res, has_h, ctx_row):
    it = iter(refs)
    if split_in:
        ctx_ref, x_ref = next(it), next(it)
        s = jnp.where(pl.program_id(1) == 0, ctx_ref[0], x_ref[0])
    else:
        s = next(it)[0]
    if has_res:
        o_ref, gate_ref, post_ref = next(it), next(it), next(it)
    if has_h:
        pre_ref, shift_ref, scale_ref = next(it), next(it), next(it)
    if has_res:
        s = s + _mod_row(gate_ref, ctx_row) * _rms(o_ref[0].astype(F32), post_ref[...])
    if has_res or split_in:
        next(it)[0] = s
    if has_h:
        h = _rms(s, pre_ref[...]) * (1.0 + _mod_row(scale_ref, ctx_row)) + _mod_row(shift_ref, ctx_row)
        next(it)[0] = h.astype(BF16)


def _res_norm(s, mod, *, o=None, gate=None, post_w=None, pre_w=None, shift=None, scale=None,
              s_off=0, o_off=0, n_tiles=None, has_ctx=True):
    split_in = isinstance(s, tuple)
    b, _, d = (s[1] if split_in else s).shape
    has_res = o is not None
    has_h = pre_w is not None
    if n_tiles is None:
        n_tiles = (s[0].shape[1] + s[1].shape[1]) // ROW_TILE if split_in else s.shape[1] // ROW_TILE - s_off
    ctx_row = b if has_ctx else None

    def outrows(bi, j):
        return (bi, j, 0)

    def mod_spec(sel):
        layer, comp = sel
        return pl.BlockSpec((1, 8, d), lambda bi, j: (layer, 0, comp))

    vec_spec = pl.BlockSpec((1, d), lambda bi, j: (0, 0))
    tile = (1, ROW_TILE, d)
    if split_in:
        assert s[0].shape[1] == ROW_TILE and has_ctx and not has_res
        args = list(s)
        in_specs = [pl.BlockSpec(tile, lambda bi, j: (bi, 0, 0)),
                    pl.BlockSpec(tile, lambda bi, j: (bi, jnp.maximum(j - 1, 0), 0))]
    else:
        args, in_specs = [s], [pl.BlockSpec(tile, lambda bi, j: (bi, j + s_off, 0))]
    out_specs, out_shape = [], []
    if has_res:
        args += [o, mod, post_w.reshape(1, d)]
        in_specs += [pl.BlockSpec(tile, lambda bi, j: (bi, j + o_off, 0)), mod_spec(gate), vec_spec]
    if has_res or split_in:
        out_specs.append(pl.BlockSpec(tile, outrows))
        out_shape.append(jax.ShapeDtypeStruct((b, n_tiles * ROW_TILE, d), F32))
    if has_h:
        args += [pre_w.reshape(1, d), mod, mod]
        in_specs += [vec_spec, mod_spec(shift), mod_spec(scale)]
        out_specs.append(pl.BlockSpec(tile, outrows))
        out_shape.append(jax.ShapeDtypeStruct((b, n_tiles * ROW_TILE, d), BF16))
    outs = pl.pallas_call(
        functools.partial(_res_norm_kernel, split_in=split_in, has_res=has_res, has_h=has_h, ctx_row=ctx_row),
        grid=(b, n_tiles),
        in_specs=in_specs,
        out_specs=out_specs,
        out_shape=out_shape,
        compiler_params=_cparams(("parallel", "parallel")),
        name="res_norm",
    )(*args)
    outs = list(outs)
    s_new = outs.pop(0) if (has_res or split_in) else None
    h = outs.pop(0) if has_h else None
    return s_new, h


MM_TM = 2048


def _row_tile(m, cap=MM_TM):
    best = ROW_TILE
    for tm in range(ROW_TILE, cap + 1, ROW_TILE):
        if m % tm == 0:
            best = tm
    return best


def _mm_kernel(a_ref, *refs, swiglu, w_transposed, has_side):
    a = a_ref[...]
    dot = _dot_nt if w_transposed else _dot
    if has_side:
        *refs, side_o_ref = refs
        side_w_ref = refs.pop(-2)

        @pl.when(pl.program_id(1) == 0)
        def _():
            side_o_ref[...] = dot(a, side_w_ref[0].astype(BF16))

    if swiglu:
        w1_ref, w3_ref, o_ref = refs
        g = dot(a, w1_ref[0].astype(BF16))
        u = dot(a, w3_ref[0].astype(BF16))
        o_ref[...] = (_silu(g) * u).astype(o_ref.dtype)
    else:
        w_ref, o_ref = refs
        o_ref[...] = dot(a, w_ref[0].astype(BF16)).astype(o_ref.dtype)


def _matmul(a, ws, *, layer=0, col_off=0, n, tn, out_dtype, w_transposed=False, side_w=None):
    m, k = a.shape
    swiglu = len(ws) == 2
    has_side = side_w is not None
    tm = _row_tile(m)
    assert col_off % tn == 0
    off = col_off // tn
    if w_transposed:
        w_spec = pl.BlockSpec((1, tn, k), lambda i, j: (layer, j + off, 0))
    else:
        w_spec = pl.BlockSpec((1, k, tn), lambda i, j: (layer, 0, j + off))
    in_specs = [pl.BlockSpec((tm, k), lambda i, j: (i, 0), pipeline_mode=pl.Buffered(1))] + [w_spec] * len(ws)
    out_specs = [pl.BlockSpec((tm, tn), lambda i, j: (i, j))]
    out_shape = [jax.ShapeDtypeStruct((m, n), out_dtype)]
    args = [a, *ws]
    if has_side:
        ns = side_w.shape[1] if w_transposed else side_w.shape[2]
        in_specs.append(pl.BlockSpec(side_w.shape, lambda i, j: (0, 0, 0), pipeline_mode=pl.Buffered(1)))
        out_specs.append(pl.BlockSpec((tm, ns), lambda i, j: (i, 0)))
        out_shape.append(jax.ShapeDtypeStruct((m, ns), F32))
        args.append(side_w)
    outs = pl.pallas_call(
        functools.partial(_mm_kernel, swiglu=swiglu, w_transposed=w_transposed, has_side=has_side),
        grid=(m // tm, pl.cdiv(n, tn)),
        in_specs=in_specs,
        out_specs=out_specs,
        out_shape=out_shape,
        compiler_params=_cparams(("parallel", "arbitrary")),
        name="matmul_swiglu" if swiglu else "matmul",
    )(*args)
    return tuple(outs) if has_side else outs[0]


MMK_TN = 1024
MMK_TK = 1024


def _mm_ktiled_kernel(a_ref, w_ref, o_ref, acc_ref, *, k_total, tk):
    kk = pl.program_id(2)
    nk = pl.num_programs(2)

    @pl.when(kk == 0)
    def _():
        acc_ref[...] = jnp.zeros_like(acc_ref)

    rem = k_total - (pl.cdiv(k_total, tk) - 1) * tk

    def step(a, w):
        acc_ref[...] += _dot(a, w.astype(BF16))

    if rem == tk:
        step(a_ref[...], w_ref[0])
    else:
        @pl.when(kk < nk - 1)
        def _():
            step(a_ref[...], w_ref[0])

        @pl.when(kk == nk - 1)
        def _():
            a = a_ref[...]
            w = w_ref[0]
            ka = lax.broadcasted_iota(jnp.int32, a.shape, 1)
            kw = lax.broadcasted_iota(jnp.int32, w.shape, 0)
            step(jnp.where(ka < rem, a, jnp.zeros_like(a)), jnp.where(kw < rem, w, jnp.zeros_like(w)))

    @pl.when(kk == nk - 1)
    def _():
        o_ref[...] = acc_ref[...].astype(o_ref.dtype)


def _matmul_ktiled(a, w, *, layer, tk=MMK_TK):
    m, k = a.shape
    n = w.shape[2]
    tm = _row_tile(m)
    tn = min(MMK_TN, n)
    assert n % tn == 0
    return pl.pallas_call(
        functools.partial(_mm_ktiled_kernel, k_total=k, tk=tk),
        grid=(m // tm, n // tn, pl.cdiv(k, tk)),
        in_specs=[pl.BlockSpec((tm, tk), lambda i, j, kk: (i, kk)),
                  pl.BlockSpec((1, tk, tn), lambda i, j, kk: (layer, kk, j))],
        out_specs=pl.BlockSpec((tm, tn), lambda i, j, kk: (i, j)),
        out_shape=jax.ShapeDtypeStruct((m, n), BF16),
        scratch_shapes=[pltpu.VMEM((tm, tn), F32)],
        compiler_params=_cparams(("parallel", "parallel", "arbitrary")),
        name="matmul_ktiled",
    )(a, w)


CONV_CW = 1024


def _conv_kernel(cur_ref, prev_ref, next_ref, w_ref, *refs, has_bias, l2_scaled_blocks, l2_scale):
    if has_bias:
        b_ref, o_ref = refs
    else:
        (o_ref,) = refs
    j = pl.program_id(1)
    nj = pl.num_programs(1)
    tr = cur_ref.shape[1]
    use_prev = jnp.logical_and(j != 0, j != 1)
    use_next = jnp.logical_and(j != 0, j != nj - 1)
    zeros = jnp.zeros(prev_ref.shape[1:], F32)
    xext = jnp.concatenate([jnp.where(use_prev, prev_ref[0].astype(F32), zeros),
                            cur_ref[0].astype(F32),
                            jnp.where(use_next, next_ref[0].astype(F32), zeros)], axis=0)
    n_ext = xext.shape[0]
    acc = None
    for t in range(CONV_W):
        shift = (CONV_W // 2 - t) % n_ext
        tap = pltpu.roll(xext, shift, axis=0) if shift else xext
        term = w_ref[t:t + 1, :] * tap[CONV_HALO:CONV_HALO + tr, :]
        acc = term if acc is None else acc + term
    if has_bias:
        acc = acc + b_ref[...]
    y = _silu(acc)
    if l2_scaled_blocks is None:
        o_ref[0] = y.astype(o_ref.dtype)
    else:
        scale = jnp.where(pl.program_id(2) < l2_scaled_blocks, l2_scale, 1.0).astype(F32)
        for hd in range(y.shape[1] // LANES):
            sl = slice(hd * LANES, (hd + 1) * LANES)
            yh = y[:, sl]
            ss = jnp.sum(yh * yh, axis=-1, keepdims=True)
            o_ref[0, :, sl] = (yh * (lax.rsqrt(ss + EPS) * scale)).astype(o_ref.dtype)


def _conv_silu(proj, conv_w, conv_b, *, col_off, width, l2_scaled_blocks=None, l2_scale=1.0):
    b, l, _ = proj.shape
    assert col_off % CONV_CW == 0 and width % CONV_CW == 0 and l % ROW_TILE == 0
    coff = col_off // CONV_CW
    hpt = ROW_TILE // CONV_HALO
    n_halo = l // CONV_HALO
    has_bias = conv_b is not None
    args = [proj, proj, proj, conv_w]
    in_specs = [
        pl.BlockSpec((1, ROW_TILE, CONV_CW), lambda bi, j, c: (bi, j, c + coff)),
        pl.BlockSpec((1, CONV_HALO, CONV_CW), lambda bi, j, c: (bi, jnp.maximum(j * hpt - 1, 0), c + coff)),
        pl.BlockSpec((1, CONV_HALO, CONV_CW),
                     lambda bi, j, c: (bi, jnp.minimum((j + 1) * hpt, n_halo - 1), c + coff)),
        pl.BlockSpec((CONV_W, CONV_CW), lambda bi, j, c: (0, c)),
    ]
    if has_bias:
        args.append(conv_b.reshape(1, width))
        in_specs.append(pl.BlockSpec((1, CONV_CW), lambda bi, j, c: (0, c)))
    return pl.pallas_call(
        functools.partial(_conv_kernel, has_bias=has_bias, l2_scaled_blocks=l2_scaled_blocks, l2_scale=l2_scale),
        grid=(b, l // ROW_TILE, width // CONV_CW),
        in_specs=in_specs,
        out_specs=pl.BlockSpec((1, ROW_TILE, CONV_CW), lambda bi, j, c: (bi, j, c)),
        out_shape=jax.ShapeDtypeStruct((b, l, width), BF16),
        compiler_params=_cparams(("parallel", "parallel", "parallel")),
        name="conv_silu",
    )(*args)


def _chunk_of(d, t, n_ctx, n_all):
    rev = jnp.where(t < n_ctx, n_ctx - 1 - t, n_all - 1 - (t - n_ctx))
    return jnp.where(d == 0, t, rev)


def _scan_masks(d, reps=1):
    ii = lax.broadcasted_iota(jnp.int32, (CHUNK, reps * CHUNK), 0)
    jj = lax.broadcasted_iota(jnp.int32, (CHUNK, reps * CHUNK), 1) % CHUNK
    fwd = d == 0
    later = jnp.where(fwd, ii, jj)
    earlier = jnp.where(fwd, jj, ii)
    incl = earlier <= later
    strict = earlier < later
    return ii, jj, fwd, incl, strict


def _even_scan_kernel(xs_ref, bc_ref, dt_ref, rq_ref, rk_ref, rv_ref, cos_ref, sin_ref,
                      dtb_ref, alog_ref, rdec_ref, yssd_ref, yret_ref, hs_ref, hr_ref):
    d = pl.program_id(1)
    t = pl.program_id(2)

    @pl.when(t == 0)
    def _():
        hs_ref[...] = jnp.zeros_like(hs_ref)
        hr_ref[...] = jnp.zeros_like(hr_ref)

    ii, jj, fwd, incl, _ = _scan_masks(d)
    lane = lax.broadcasted_iota(jnp.int32, (CHUNK, LANES), 1)
    row = lax.broadcasted_iota(jnp.int32, (CHUNK, LANES), 0)
    lane_lo = lane < (LANES // 2)
    row_lo = row < (CHUNK // 2)

    dtf = _softplus(dt_ref[0] + dtb_ref[0])
    la = -jnp.exp(alog_ref[0]) * dtf
    tri = jnp.where(incl, 1.0, 0.0).astype(BF16)
    cs = _dot_sel_left(tri, la)
    tot = jnp.sum(la, axis=0, keepdims=True)
    cs_t = cs.T
    dt_t = dtf.T
    wend_t = (dtf * jnp.exp(tot - cs)).T

    bc = bc_ref[0].astype(F32)
    n_state = SSD_GROUPS * SSD_STATE
    for g in range(SSD_GROUPS):
        bm = bc[:, g * SSD_STATE:(g + 1) * SSD_STATE]
        cm = bc[:, n_state + g * SSD_STATE:n_state + (g + 1) * SSD_STATE]
        bm_bf = bm.astype(BF16)
        cm_bf = cm.astype(BF16)
        scores = _dot_nt(cm_bf, bm_bf)
        bm_t = bm.T
        heads_per_group = SSD_HEADS // SSD_GROUPS
        for pp in range(heads_per_group // 2):
            pidx = g * (heads_per_group // 2) + pp
            xs_pair = xs_ref[0, :, pidx * LANES:(pidx + 1) * LANES]
            x_lo = jnp.where(lane_lo, xs_pair, 0.0).astype(BF16)
            x_hi = jnp.where(lane_lo, 0.0, xs_pair).astype(BF16)
            h_pair = hs_ref[pidx]
            h_lo = jnp.where(lane_lo, h_pair, 0.0).astype(BF16)
            h_hi = jnp.where(lane_lo, 0.0, h_pair).astype(BF16)
            ms, qs, kws, ets = [], [], [], []
            for r in (2 * pidx, 2 * pidx + 1):
                cs_b = jnp.broadcast_to(cs[:, r:r + 1], (CHUNK, LANES))
                dec = jnp.exp(jnp.where(incl, cs_b - cs_t[r:r + 1, :], NEG_BIG))
                ms.append((scores * dec * dt_t[r:r + 1, :]).astype(BF16))
                qs.append((cm * jnp.exp(cs_b)).astype(BF16))
                kws.append((bm_t * wend_t[r:r + 1, :]).astype(BF16))
                ets.append(jnp.exp(jnp.where(fwd, cs_b[CHUNK - 1:CHUNK, :], cs_b[0:1, :])))
            lhs = jnp.concatenate(ms + qs, axis=1)
            rhs = jnp.concatenate([x_lo, x_hi, h_lo, h_hi], axis=0)
            yssd_ref[0, 0, :, pidx * LANES:(pidx + 1) * LANES] = _dot(lhs, rhs).astype(yssd_ref.dtype)
            et = jnp.where(lane_lo, jnp.broadcast_to(ets[0], (CHUNK, LANES)),
                           jnp.broadcast_to(ets[1], (CHUNK, LANES)))
            upd = _dot(jnp.concatenate(kws, axis=1), jnp.concatenate([x_lo, x_hi], axis=0))
            hs_ref[pidx] = h_pair * et + upd

    lg = jax.nn.log_sigmoid(rdec_ref[0])
    dist = jnp.where(fwd, ii - jj, jj - ii).astype(F32)
    cnt = jnp.where(fwd, ii + 1, CHUNK - ii).astype(F32)
    rem_t = jnp.where(fwd, CHUNK - 1 - jj, jj).astype(F32)
    cos = cos_ref[...]
    sin = sin_ref[...]
    half = RET_QK_DIM // 2
    swap_lo = (lane % RET_QK_DIM) < half

    def rotary(tt):
        swapped = jnp.where(swap_lo, pltpu.roll(tt, LANES - half, axis=1), pltpu.roll(tt, half, axis=1))
        return tt * cos + swapped * sin

    for p in range(RET_HEADS // 2):
        sl = slice(p * LANES, (p + 1) * LANES)
        q_pair = rotary(rq_ref[0, :, sl].astype(F32))
        k_pair = rotary(rk_ref[0, :, sl].astype(F32) * (RET_QK_DIM ** -0.5))
        k_pair_bf = k_pair.astype(BF16)
        k_t = k_pair.T
        h_pair = hr_ref[p]
        h_bf = h_pair.astype(BF16)
        kts, vs, es = [], [], []
        for side, hh in enumerate((2 * p, 2 * p + 1)):
            lgb = jnp.broadcast_to(lg[hh:hh + 1, :], (CHUNK, LANES))
            keep = lane_lo if side == 0 else jnp.logical_not(lane_lo)
            keep_rows = row_lo if side == 0 else jnp.logical_not(row_lo)
            q_h = jnp.where(keep, q_pair, 0.0)
            scores = _dot_nt(q_h.astype(BF16), k_pair_bf)
            dec = jnp.where(incl, jnp.exp(lgb * dist), 0.0)
            m = (scores * dec).astype(BF16)
            qe = (q_h * jnp.exp(lgb * cnt)).astype(BF16)
            v_h = rv_ref[0, :, hh * RET_V_DIM:(hh + 1) * RET_V_DIM].astype(BF16)
            y = _dot(jnp.concatenate([m, qe], axis=1), jnp.concatenate([v_h, h_bf], axis=0))
            yret_ref[0, 0, :, hh * RET_V_DIM:(hh + 1) * RET_V_DIM] = y.astype(yret_ref.dtype)
            kts.append(jnp.where(keep_rows, k_t * jnp.exp(lgb * rem_t), 0.0).astype(BF16))
            vs.append(v_h)
            es.append(jnp.exp(lgb * float(CHUNK)))
        upd = _dot(jnp.concatenate(kts, axis=1), jnp.concatenate(vs, axis=0))
        hr_ref[p] = h_pair * jnp.where(row_lo, es[0], es[1]) + upd


def _even_scan(xbc, dtp, proj, cos_t, sin_t, dt_bias, a_log, ret_decay, *, n_ctx, rq_blk, rk_blk, rv_blk):
    b, l, _ = xbc.shape
    nc = l // CHUNK
    cmap = functools.partial(_chunk_of, n_ctx=n_ctx, n_all=nc)
    out_spec = pl.BlockSpec((1, 1, CHUNK, SSD_INNER), lambda bi, d, t: (bi, d, cmap(d, t), 0))
    return pl.pallas_call(
        _even_scan_kernel,
        grid=(b, 2, nc),
        in_specs=[
            pl.BlockSpec((1, CHUNK, SSD_INNER), lambda bi, d, t: (bi, cmap(d, t), 0)),
            pl.BlockSpec((1, CHUNK, 2 * SSD_GROUPS * SSD_STATE), lambda bi, d, t: (bi, cmap(d, t), 2)),
            pl.BlockSpec((1, CHUNK, LANES), lambda bi, d, t: (bi, cmap(d, t), d)),
            pl.BlockSpec((1, CHUNK, RET_QK), lambda bi, d, t: (bi, cmap(d, t), rq_blk)),
            pl.BlockSpec((1, CHUNK, RET_QK), lambda bi, d, t: (bi, cmap(d, t), rk_blk)),
            pl.BlockSpec((1, CHUNK, RET_V), lambda bi, d, t: (bi, cmap(d, t), rv_blk)),
            pl.BlockSpec((CHUNK, LANES), lambda bi, d, t: (cmap(d, t), 0)),
            pl.BlockSpec((CHUNK, LANES), lambda bi, d, t: (cmap(d, t), 0)),
            pl.BlockSpec((1, 1, LANES), lambda bi, d, t: (d, 0, 0)),
            pl.BlockSpec((1, 1, LANES), lambda bi, d, t: (d, 0, 0)),
            pl.BlockSpec((1, RET_HEADS, LANES), lambda bi, d, t: (d, 0, 0)),
        ],
        out_specs=[out_spec, out_spec],
        out_shape=[jax.ShapeDtypeStruct((b, 2, l, SSD_INNER), BF16),
                   jax.ShapeDtypeStruct((b, 2, l, RET_V), BF16)],
        scratch_shapes=[pltpu.VMEM((SSD_HEADS // 2, SSD_STATE, LANES), F32),
                        pltpu.VMEM((RET_HEADS // 2, LANES, RET_V_DIM), F32)],
        compiler_params=_cparams(("parallel", "parallel", "arbitrary")),
        name="even_scan",
    )(xbc, xbc, dtp, proj, proj, proj, cos_t, sin_t, dt_bias, a_log, ret_decay)


def _even_out_kernel(ys0_ref, ys1_ref, yr0_ref, yr1_ref, xs_ref, z_ref, rg_ref, dsk_ref, nw_ref, o_ref):
    def f32(ref, *idx):
        return ref[idx].astype(F32)

    y = (f32(ys0_ref, 0, 0) + f32(ys1_ref, 0, 0) + dsk_ref[...] * f32(xs_ref, 0)) * _silu(f32(z_ref, 0))
    gsz = SSD_INNER // SSD_GROUPS
    for g in range(SSD_GROUPS):
        sl = slice(g * gsz, (g + 1) * gsz)
        o_ref[0, :, sl] = _rms(y[:, sl], nw_ref[:, sl]).astype(BF16)
    yr = f32(yr0_ref, 0, 0) + f32(yr1_ref, 0, 0)
    gate = _silu(f32(rg_ref, 0))
    for hh in range(RET_HEADS):
        sl = slice(hh * RET_V_DIM, (hh + 1) * RET_V_DIM)
        v = yr[:, sl]
        mu = jnp.mean(v, axis=-1, keepdims=True)
        cen = v - mu
        var = jnp.mean(cen * cen, axis=-1, keepdims=True)
        o_ref[0, :, SSD_INNER + hh * RET_V_DIM:SSD_INNER + (hh + 1) * RET_V_DIM] = (
            cen * lax.rsqrt(var + EPS) * gate[:, sl]).astype(BF16)


EVEN_OUT_TILE = ROW_TILE


def _even_out(y_ssd, y_ret, xbc, z_src, rg_src, d_skip, ssd_norm, *, z_blk, rg_blk):
    b, _, l, _ = y_ssd.shape
    tr = EVEN_OUT_TILE
    w = SSD_INNER
    dsk = jnp.repeat(d_skip, SSD_HEAD_DIM).reshape(1, w)

    def dir_spec(dd):
        return pl.BlockSpec((1, 1, tr, w), lambda bi, j: (bi, dd, j, 0))

    def col_spec(blk):
        return pl.BlockSpec((1, tr, w), lambda bi, j: (bi, j, blk))

    vec = pl.BlockSpec((1, w), lambda bi, j: (0, 0))
    return pl.pallas_call(
        _even_out_kernel,
        grid=(b, l // tr),
        in_specs=[dir_spec(0), dir_spec(1), dir_spec(0), dir_spec(1), col_spec(0), col_spec(z_blk),
                  col_spec(rg_blk), vec, vec],
        out_specs=pl.BlockSpec((1, tr, 2 * w), lambda bi, j: (bi, j, 0)),
        out_shape=jax.ShapeDtypeStruct((b, l, 2 * w), BF16),
        compiler_params=_cparams(("parallel", "parallel")),
        name="even_out",
    )(y_ssd, y_ssd, y_ret, y_ret, xbc, z_src, rg_src, dsk, ssd_norm.reshape(1, w))


GDN_LEVELS = 7
GDN_GROUP = 16


def _bdot(a, b):
    return jnp.einsum("gmk,gkn->gmn", a, b, preferred_element_type=F32)


def _bdot_nt(a, b):
    return jnp.einsum("gmk,gnk->gmn", a, b, preferred_element_type=F32)


def _gdn_scan_kernel(q_ref, k_ref, v_ref, raw_ref, bias_ref, alog_ref, lvl_ref, o_ref, s_ref):
    d = pl.program_id(1)
    t = pl.program_id(2)

    @pl.when(t == 0)
    def _():
        s_ref[...] = jnp.zeros_like(s_ref)

    ii, jj, fwd, incl, strict = _scan_masks(d)
    strict_f = jnp.where(strict, 1.0, 0.0)
    eye = jnp.where(ii == jj, 1.0, 0.0)
    tri = jnp.where(incl, 1.0, 0.0).astype(BF16)

    raw = raw_ref[0]
    beta = jax.nn.sigmoid(raw)
    gl = -jnp.exp(alog_ref[0]) * _softplus(raw + bias_ref[0])
    cs = _dot_sel_left(tri, gl)
    tot = jnp.sum(gl, axis=0, keepdims=True)
    cs_t = cs.T
    wend_t = jnp.exp(tot - cs).T

    gsz = GDN_GROUP
    rep = GDN_V_HEADS // GDN_K_HEADS
    kper = gsz // rep
    hd = GDN_HEAD_DIM
    strict_f3 = strict_f[None]
    incl3 = incl[None]

    def col_bcast(a, c):
        return jnp.broadcast_to(a[:, c:c + 1], (CHUNK, LANES))

    def per_key_head(a4, x):
        res = a4[:, None] * x.reshape((kper, rep) + x.shape[1:])
        return res.reshape((gsz,) + res.shape[2:])

    for gi in range(GDN_V_HEADS // gsz):
        heads = range(gi * gsz, (gi + 1) * gsz)
        kheads = range(gi * kper, (gi + 1) * kper)
        q4b = jnp.stack([q_ref[0, :, kh * hd:(kh + 1) * hd] for kh in kheads])
        k4b = jnp.stack([k_ref[0, :, kh * hd:(kh + 1) * hd] for kh in kheads])
        k4 = k4b.astype(F32)
        kk4 = _bdot_nt(k4b, k4b)
        qk4 = _bdot_nt(q4b, k4b)
        kt4 = jnp.stack([k4[i].T for i in range(kper)])
        v = jnp.stack([v_ref[0, :, h * hd:(h + 1) * hd] for h in heads])
        cs_b = jnp.stack([col_bcast(cs, GDN_V_HEADS + h) for h in heads])
        beta_b = jnp.stack([col_bcast(beta, h) for h in heads])
        cs_r = jnp.stack([cs_t[GDN_V_HEADS + h:GDN_V_HEADS + h + 1, :] for h in heads])
        wend_r = jnp.stack([wend_t[GDN_V_HEADS + h:GDN_V_HEADS + h + 1, :] for h in heads])
        gam = jnp.exp(jnp.where(incl3, cs_b - cs_r, NEG_BIG))
        ecs_b = jnp.exp(cs_b)
        etot = jnp.exp(jnp.where(fwd, cs_b[:, CHUNK - 1:CHUNK, :], cs_b[:, 0:1, :]))
        low = per_key_head(kk4, gam * (strict_f3 * beta_b))
        tinv = eye[None] - low * lvl_ref[0][None]
        for lv in range(1, GDN_LEVELS):
            a_off = (low * lvl_ref[lv][None]).astype(BF16)
            x = _bdot(a_off, tinv.astype(BF16))
            tinv = tinv - _bdot(tinv.astype(BF16), x.astype(BF16))
        state = s_ref[gi * gsz:(gi + 1) * gsz]
        ke = per_key_head(k4, ecs_b).astype(BF16)
        qe = per_key_head(q4b.astype(F32), ecs_b).astype(BF16)
        kq_s = _bdot(jnp.concatenate([ke, qe], axis=1), state.astype(BF16))
        resid = (beta_b * (v.astype(F32) - kq_s[:, :CHUNK])).astype(BF16)
        v_new_bf = _bdot(tinv.astype(BF16), resid).astype(BF16)
        out = kq_s[:, CHUNK:] + _bdot(per_key_head(qk4, gam).astype(BF16), v_new_bf)
        for g, h in enumerate(heads):
            o_ref[0, 0, :, h * hd:(h + 1) * hd] = out[g].astype(o_ref.dtype)
        k_end_t = per_key_head(kt4, wend_r).astype(BF16)
        s_ref[gi * gsz:(gi + 1) * gsz] = state * etot + _bdot(k_end_t, v_new_bf)


def _gdn_level_masks():
    i = jnp.arange(CHUNK)[:, None]
    j = jnp.arange(CHUNK)[None, :]
    masks = []
    for lv in range(GDN_LEVELS):
        blk = 1 << lv
        masks.append(((i // (2 * blk) == j // (2 * blk)) & (i // blk != j // blk)).astype(F32))
    return jnp.stack(masks)


def _gdn_scan(qk, v, raw, bias, a_log, *, n_ctx):
    b, l, _ = qk.shape
    nc = l // CHUNK
    cmap = functools.partial(_chunk_of, n_ctx=n_ctx, n_all=nc)
    return pl.pallas_call(
        _gdn_scan_kernel,
        grid=(b, 2, nc),
        in_specs=[
            pl.BlockSpec((1, CHUNK, GDN_K), lambda bi, d, t: (bi, cmap(d, t), 0)),
            pl.BlockSpec((1, CHUNK, GDN_K), lambda bi, d, t: (bi, cmap(d, t), 1)),
            pl.BlockSpec((1, CHUNK, GDN_V), lambda bi, d, t: (bi, cmap(d, t), 0)),
            pl.BlockSpec((1, CHUNK, LANES), lambda bi, d, t: (bi, cmap(d, t), d)),
            pl.BlockSpec((1, 1, LANES), lambda bi, d, t: (d, 0, 0)),
            pl.BlockSpec((1, 1, LANES), lambda bi, d, t: (d, 0, 0)),
            pl.BlockSpec((GDN_LEVELS, CHUNK, CHUNK), lambda bi, d, t: (0, 0, 0)),
        ],
        out_specs=pl.BlockSpec((1, 1, CHUNK, GDN_V), lambda bi, d, t: (bi, d, cmap(d, t), 0)),
        out_shape=jax.ShapeDtypeStruct((b, 2, l, GDN_V), BF16),
        scratch_shapes=[pltpu.VMEM((GDN_V_HEADS, GDN_HEAD_DIM, GDN_HEAD_DIM), F32)],
        compiler_params=_cparams(("parallel", "parallel", "arbitrary")),
        name="gdn_scan",
    )(qk, qk, v, raw, bias, a_log, _gdn_level_masks())


def _odd_out_kernel(o0_ref, o1_ref, z_ref, nw_ref, out_ref):
    o = o0_ref[0, 0].astype(F32) + o1_ref[0, 0].astype(F32)
    gate = _silu(z_ref[0].astype(F32))
    for hh in range(GDN_V_HEADS):
        sl = slice(hh * GDN_HEAD_DIM, (hh + 1) * GDN_HEAD_DIM)
        out_ref[0, :, sl] = (_rms(o[:, sl], nw_ref[...]) * gate[:, sl]).astype(BF16)


def _odd_out(o, proj, norm_w, *, z_blk, row_off):
    b, _, l, w = o.shape
    tr = ROW_TILE
    n_tiles = l // tr - row_off

    def dir_spec(dd):
        return pl.BlockSpec((1, 1, tr, w), lambda bi, j: (bi, dd, j + row_off, 0))

    return pl.pallas_call(
        _odd_out_kernel,
        grid=(b, n_tiles),
        in_specs=[dir_spec(0), dir_spec(1),
                  pl.BlockSpec((1, tr, w), lambda bi, j: (bi, j + row_off, z_blk)),
                  pl.BlockSpec((1, GDN_HEAD_DIM), lambda bi, j: (0, 0))],
        out_specs=pl.BlockSpec((1, tr, w), lambda bi, j: (bi, j, 0)),
        out_shape=jax.ShapeDtypeStruct((b, n_tiles * tr, w), BF16),
        compiler_params=_cparams(("parallel", "parallel")),
        name="odd_out",
    )(o, o, proj, norm_w.reshape(1, GDN_HEAD_DIM))


def _dir_lanes(p, lane_off):
    out = jnp.zeros((2, 1, LANES), F32)
    return out.at[:, 0, lane_off:lane_off + p.shape[1]].set(p.astype(F32))


def _rotary_tables(n_ctx_rows, n_lat_rows):
    nf = RET_QK_DIM // 4
    pos = jnp.arange(n_lat_rows)
    rowp = (pos // GRID_W).astype(F32)
    colp = (pos % GRID_W).astype(F32)
    freqs = ROPE_BASE ** (-jnp.arange(nf, dtype=F32) / nf)
    ang = jnp.concatenate([rowp[:, None] * freqs, colp[:, None] * freqs], axis=-1)
    cos = jnp.cos(ang)
    sin = jnp.sin(ang)
    cos_h = jnp.concatenate([cos, cos], axis=-1)
    sin_h = jnp.concatenate([-sin, sin], axis=-1)
    reps = LANES // RET_QK_DIM
    cos_t = jnp.concatenate([jnp.ones((n_ctx_rows, LANES), F32), jnp.tile(cos_h, (1, reps))], axis=0)
    sin_t = jnp.concatenate([jnp.zeros((n_ctx_rows, LANES), F32), jnp.tile(sin_h, (1, reps))], axis=0)
    return cos_t, sin_t


SSD_CONV_DIM = SSD_INNER + 2 * SSD_GROUPS * SSD_STATE
EV_HEAD = SSD_INNER + SSD_CONV_DIM
EV_TAIL_OFF = EV_HEAD + 2 * SSD_HEADS
EV_TAIL = 2 * RET_QK + 2 * RET_V
EVH_Z, EVH_XBC = 0, SSD_INNER
EVT_RQ, EVT_RK, EVT_RV, EVT_RG = 0, RET_QK, 2 * RET_QK, 2 * RET_QK + RET_V


def _even_weights(w_in_t):
    tail = w_in_t[EV_TAIL_OFF:EV_TAIL_OFF + EV_TAIL].astype(BF16)
    pad = jnp.zeros((LANES - SSD_HEADS, w_in_t.shape[1]), w_in_t.dtype)
    dt = w_in_t[EV_HEAD:EV_TAIL_OFF]
    w_dt = jnp.concatenate([dt[:SSD_HEADS], pad, dt[SSD_HEADS:], pad], axis=0)
    return tail[None], w_dt[None]


def _odd_small_weights(w_in):
    base = 2 * GDN_K + 2 * GDN_V
    hv = GDN_V_HEADS
    beta = w_in[:, base:base + 2 * hv]
    dec = w_in[:, base + 2 * hv:base + 4 * hv]
    pad = jnp.zeros((w_in.shape[0], LANES - 2 * hv), w_in.dtype)
    return jnp.concatenate([beta[:, :hv], dec[:, :hv], pad, beta[:, hv:], dec[:, hv:], pad], axis=1)[None]


def _ffn(s, o, mod, layer, post_mix, pre_ffn, w1, w3, w2, *, s_off=0):
    b, l, d = o.shape
    s1, h = _res_norm(s, mod, o=o, gate=(layer, 2), post_w=post_mix, pre_w=pre_ffn,
                      shift=(layer, 3), scale=(layer, 4), s_off=s_off, n_tiles=l // ROW_TILE,
                      has_ctx=s_off == 0)
    d_ff = w1.shape[2]
    act = _matmul(h.reshape(b * l, d), [w1, w3], layer=layer, n=d_ff, tn=256, out_dtype=BF16)
    f = _matmul_ktiled(act, w2, layer=layer)
    return s1, f.reshape(b, l, d)


def kernel(x, c, ctx, c_ctx, ada_w, ada_b, norm_mix_pre, norm_mix_post, norm_ffn_pre, norm_ffn_post,
           ev_w_in, ev_conv_w, ev_conv_b, ev_dt_bias, ev_a_log, ev_d_skip, ev_ssd_norm, ev_ret_decay,
           ev_w_out, od_w_in, od_conv_w, od_dt_bias, od_a_log, od_norm, od_w_out,
           ffn_w1, ffn_w3, ffn_w2):
    b, seq, d = x.shape
    n_ctx_rows = ctx.shape[1]
    assert n_ctx_rows == ROW_TILE and seq % ROW_TILE == 0
    l = n_ctx_rows + seq
    m = b * l
    n_ctx = n_ctx_rows // CHUNK

    mod = _ada(jnp.concatenate([c, c_ctx[None]], axis=0), ada_w, ada_b)

    s, h = _res_norm((ctx, x), mod, pre_w=norm_mix_pre[0], shift=(0, 0), scale=(0, 1))
    ev_w_in_t = jnp.swapaxes(ev_w_in, 1, 2)
    w_tail, w_dt = _even_weights(ev_w_in_t[0])
    h2 = h.reshape(m, d)
    proj_h = _matmul(h2, [ev_w_in_t], layer=0, n=EV_HEAD, tn=512, out_dtype=BF16,
                     w_transposed=True).reshape(b, l, EV_HEAD)
    proj_t, dtp = _matmul(h2, [w_tail], n=EV_TAIL, tn=512, out_dtype=BF16, w_transposed=True,
                          side_w=w_dt.astype(BF16))
    proj_t = proj_t.reshape(b, l, EV_TAIL)
    dtp = dtp.reshape(b, l, 2 * LANES)
    xbc = _conv_silu(proj_h, ev_conv_w[0], ev_conv_b[0], col_off=EVH_XBC, width=SSD_CONV_DIM)
    cos_t, sin_t = _rotary_tables(n_ctx_rows, seq)
    rdec = jnp.broadcast_to(ev_ret_decay[0][:, :, None], (2, RET_HEADS, LANES)).astype(F32)
    y_ssd, y_ret = _even_scan(xbc, dtp, proj_t, cos_t, sin_t, _dir_lanes(ev_dt_bias[0], 0),
                              _dir_lanes(ev_a_log[0], 0), rdec, n_ctx=n_ctx,
                              rq_blk=EVT_RQ // RET_QK, rk_blk=EVT_RK // RET_QK, rv_blk=EVT_RV // RET_V)
    act = _even_out(y_ssd, y_ret, xbc, proj_h, proj_t, ev_d_skip[0], ev_ssd_norm[0],
                    z_blk=EVH_Z // SSD_INNER, rg_blk=EVT_RG // SSD_INNER)
    o = _matmul(act.reshape(m, act.shape[-1]), [ev_w_out], layer=0, n=d, tn=512, out_dtype=BF16).reshape(b, l, d)
    s, f = _ffn(s, o, mod, 0, norm_mix_post[0], norm_ffn_pre[0], ffn_w1, ffn_w3, ffn_w2)

    s, h = _res_norm(s, mod, o=f, gate=(0, 5), post_w=norm_ffn_post[0], pre_w=norm_mix_pre[1],
                     shift=(1, 0), scale=(1, 1))
    h2 = h.reshape(m, d)
    n_main = 2 * GDN_K + 2 * GDN_V
    proj, raw = _matmul(h2, [od_w_in], layer=0, n=n_main, tn=512, out_dtype=BF16,
                        side_w=_odd_small_weights(od_w_in[0]).astype(BF16))
    proj = proj.reshape(b, l, n_main)
    raw = raw.reshape(b, l, 2 * LANES)
    qk = _conv_silu(proj, od_conv_w[0][:, :2 * GDN_K], None, col_off=0, width=2 * GDN_K,
                    l2_scaled_blocks=GDN_K // CONV_CW, l2_scale=GDN_HEAD_DIM ** -0.5)
    v = _conv_silu(proj, od_conv_w[0][:, 2 * GDN_K:], None, col_off=2 * GDN_K, width=GDN_V)
    o2 = _gdn_scan(qk, v, raw, _dir_lanes(od_dt_bias[0], GDN_V_HEADS), _dir_lanes(od_a_log[0], GDN_V_HEADS),
                   n_ctx=n_ctx)
    ctx_tiles = n_ctx_rows // ROW_TILE
    act = _odd_out(o2, proj, od_norm[0], z_blk=(2 * GDN_K + GDN_V) // GDN_V, row_off=ctx_tiles)
    o = _matmul(act.reshape(b * seq, act.shape[-1]), [od_w_out], layer=0, n=d, tn=512,
                out_dtype=BF16).reshape(b, seq, d)
    s, f = _ffn(s, o, mod, 1, norm_mix_post[1], norm_ffn_pre[1], ffn_w1, ffn_w3, ffn_w2, s_off=ctx_tiles)
    out, _ = _res_norm(s, mod, o=f, gate=(1, 5), post_w=norm_ffn_post[1], has_ctx=False)
    return out
```
